```python
import jax, jax.numpy as jnp
from jax import lax
import numpy as np

D_MODEL = 1024
BATCH = 16
SEQ = 4096
DEPTH = 1

GRID_W = 64
MIX_WIDTH = D_MODEL
NA_WIDTH = MIX_WIDTH // 2
HG_WIDTH = MIX_WIDTH - NA_WIDTH
NA_HEADS = 8
NA_HEAD_DIM = NA_WIDTH // NA_HEADS
NA_ROWS = 8
NA_COLS = 16
NA_QCB = 16
NA_KCB = 2 * NA_QCB
HG_HEAD_DIM = 128
HG_HEADS = HG_WIDTH // HG_HEAD_DIM
HG_CHUNK = 64
PLE_DIM = 256
NORM_EPS = 1e-6
PROJ_WIDTHS = (NA_WIDTH, NA_WIDTH, NA_WIDTH, NA_WIDTH,
               HG_WIDTH, HG_WIDTH, HG_WIDTH, HG_WIDTH, HG_WIDTH)
PROJ_TOTAL = sum(PROJ_WIDTHS)

kernel_name = "hymba_natten_hgrn2_encoder_layer"


def rms_norm(x, g):
    xf = x.astype(jnp.float32)
    y = xf * lax.rsqrt(jnp.mean(xf * xf, axis=-1, keepdims=True) + NORM_EPS)
    return (y * g.astype(jnp.float32)).astype(x.dtype)


def group_rms_norm(x, g, n_heads):
    B, S, W = x.shape
    xh = x.reshape(B, S, n_heads, W // n_heads)
    gh = g.reshape(n_heads, W // n_heads)
    return rms_norm(xh, gh).reshape(B, S, W)


def neighborhood_attention(q, k, v, rpb):
    B, S, _ = q.shape
    rows = S // GRID_W
    kh = min(NA_ROWS, rows)

    def to_grid(t):
        return t.reshape(B, rows, GRID_W, NA_HEADS, NA_HEAD_DIM).transpose(0, 3, 1, 2, 4)

    qg, kg, vg = to_grid(q), to_grid(k), to_grid(v)

    ncb = GRID_W // NA_QCB
    cols = np.arange(GRID_W)
    col_start = np.clip(cols - NA_COLS // 2, 0, GRID_W - NA_COLS)
    band_start = np.clip(np.arange(ncb) * NA_QCB - NA_COLS // 2, 0, GRID_W - NA_KCB)
    band_idx = band_start[:, None] + np.arange(NA_KCB)[None, :]
    qcol = cols.reshape(ncb, NA_QCB)[:, :, None]
    kcol = band_idx[:, None, :]
    cs = col_start.reshape(ncb, NA_QCB)[:, :, None]
    col_mask = (kcol >= cs) & (kcol < cs + NA_COLS)
    dc_idx = np.clip(kcol - qcol + NA_COLS - 1, 0, 2 * NA_COLS - 2)
    scale = NA_HEAD_DIM ** -0.5

    def row_block(r):
        rs = jnp.clip(r - kh // 2, 0, rows - kh)
        k_rows = lax.dynamic_slice_in_dim(kg, rs, kh, axis=2)
        v_rows = lax.dynamic_slice_in_dim(vg, rs, kh, axis=2)
        k_band = k_rows[:, :, :, band_idx, :]
        v_band = v_rows[:, :, :, band_idx, :]
        q_row = lax.dynamic_index_in_dim(qg, r, axis=2, keepdims=False)
        q_row = q_row.reshape(B, NA_HEADS, ncb, NA_QCB, NA_HEAD_DIM)
        s = jnp.einsum('bhnqd,bhinjd->bhnqij', q_row, k_band).astype(jnp.float32) * scale
        dr_idx = rs + jnp.arange(kh) - r + NA_ROWS - 1
        bias = rpb[:, dr_idx][:, :, dc_idx]
        bias = bias.transpose(0, 2, 3, 1, 4).astype(jnp.float32)
        s = jnp.where(col_mask[:, :, None, :], s + bias, -jnp.inf)
        shp = s.shape
        pr = jax.nn.softmax(s.reshape(shp[:-2] + (kh * NA_KCB,)), axis=-1).reshape(shp)
        o = jnp.einsum('bhnqij,bhinjd->bhnqd', pr.astype(v.dtype), v_band)
        return o.reshape(B, NA_HEADS, GRID_W, NA_HEAD_DIM)

    out = lax.map(row_block, jnp.arange(rows))
    return out.transpose(1, 0, 3, 2, 4).reshape(B, S, NA_WIDTH)


def _hgrn2_chunk_step(state, inp):
    qc, kc, vc, gc = inp
    C = qc.shape[2]
    b = jnp.cumsum(gc, axis=2)
    tri = jnp.tril(jnp.ones((C, C), dtype=bool))[:, :, None]
    inter = jnp.einsum('nhtk,nhkv->nhtv', qc * jnp.exp(b), state)
    diff = b[:, :, :, None, :] - b[:, :, None, :, :]
    decay = jnp.where(tri, jnp.exp(jnp.where(tri, diff, 0.0)), 0.0)
    scores = jnp.einsum('nhtk,nhsk,nhtsk->nhts', qc, kc, decay)
    intra = jnp.einsum('nhts,nhsv->nhtv', scores, vc)
    b_last = b[:, :, -1, :]
    new_state = jnp.exp(b_last)[..., None] * state + jnp.einsum(
        'nhsk,nhsv->nhkv', kc * jnp.exp(b_last[:, :, None, :] - b), vc)
    return new_state, inter + intra


def hgrn2_bidirectional(q_pre, f_pre_fwd, f_pre_bwd, i_in, lb):
    B, S, _ = q_pre.shape
    f32 = jnp.float32
    q = jax.nn.silu(q_pre.astype(f32))
    v = i_in.astype(f32)
    lbf = lb.astype(f32)[:, None, None, :]
    f = lbf + (1.0 - lbf) * jax.nn.sigmoid(jnp.stack([f_pre_fwd, f_pre_bwd]).astype(f32))
    logf = jnp.log(f)
    k = 1.0 - f
    qq = jnp.concatenate([q, jnp.flip(q, 1)], 0)
    kk = jnp.concatenate([k[0], jnp.flip(k[1], 1)], 0)
    vv = jnp.concatenate([v, jnp.flip(v, 1)], 0)
    gg = jnp.concatenate([logf[0], jnp.flip(logf[1], 1)], 0)
    nc = S // HG_CHUNK

    def chunks(t):
        t = t.reshape(2 * B, S, HG_HEADS, HG_HEAD_DIM).transpose(0, 2, 1, 3)
        return jnp.moveaxis(t.reshape(2 * B, HG_HEADS, nc, HG_CHUNK, HG_HEAD_DIM), 2, 0)

    state0 = jnp.zeros((2 * B, HG_HEADS, HG_HEAD_DIM, HG_HEAD_DIM), f32)
    _, o = lax.scan(_hgrn2_chunk_step, state0, (chunks(qq), chunks(kk), chunks(vv), chunks(gg)))
    o = jnp.moveaxis(o, 0, 2).reshape(2 * B, HG_HEADS, S, HG_HEAD_DIM)
    o = o[:B] + jnp.flip(o[B:], axis=2)
    return o.transpose(0, 2, 1, 3).reshape(B, S, HG_WIDTH)


def setup_inputs(seed: int = 0) -> dict:
    key = jax.random.key(seed)
    ks = jax.random.split(key, 13)
    f32 = jnp.float32
    x = jax.random.normal(ks[0], (BATCH, SEQ, D_MODEL), f32)
    p = jax.random.normal(ks[1], (DEPTH, BATCH, SEQ, PLE_DIM), f32)
    ln_mix = 1.0 + 0.02 * jax.random.normal(ks[2], (DEPTH, D_MODEL), f32)
    w_in = jax.random.normal(ks[3], (DEPTH, D_MODEL, PROJ_TOTAL), f32) * D_MODEL ** -0.5
    rpb = 0.1 * jax.random.normal(ks[4], (DEPTH, NA_HEADS, 2 * NA_ROWS - 1, 2 * NA_COLS - 1), f32)
    lb_logits = 0.5 * jax.random.normal(ks[5], (DEPTH + 1, 2, HG_WIDTH), f32)
    attn_norm = 1.0 + 0.02 * jax.random.normal(ks[6], (DEPTH, NA_WIDTH), f32)
    hgrn_norm = 1.0 + 0.02 * jax.random.normal(ks[7], (DEPTH, HG_WIDTH), f32)
    w_out = jax.random.normal(ks[8], (DEPTH, MIX_WIDTH, D_MODEL), f32) * MIX_WIDTH ** -0.5
    ln_ple = 1.0 + 0.02 * jax.random.normal(ks[9], (DEPTH, D_MODEL), f32)
    w_pg = jax.random.normal(ks[10], (DEPTH, D_MODEL, D_MODEL), f32) * D_MODEL ** -0.5
    w_pp = jax.random.normal(ks[11], (DEPTH, PLE_DIM, D_MODEL), f32) * PLE_DIM ** -0.5
    ln_final = 1.0 + 0.02 * jax.random.normal(ks[12], (D_MODEL,), f32)
    return {"x": x, "p": p, "ln_mix": ln_mix, "w_in": w_in, "rpb": rpb,
            "lb_logits": lb_logits, "attn_norm": attn_norm, "hgrn_norm": hgrn_norm,
            "w_out": w_out, "ln_ple": ln_ple, "w_pg": w_pg, "w_pp": w_pp,
            "ln_final": ln_final}


def reference(x, p, ln_mix, w_in, rpb, lb_logits, attn_norm, hgrn_norm, w_out,
              ln_ple, w_pg, w_pp, ln_final):
    lb_all = jnp.cumsum(jax.nn.softmax(lb_logits.astype(jnp.float32), axis=0), axis=0)
    split_at = list(np.cumsum(PROJ_WIDTHS)[:-1])
    h = x
    for layer in range(DEPTH):
        hn = rms_norm(h, ln_mix[layer])
        proj = hn @ w_in[layer]
        aq, ak, av, ag, hq, hf_f, hf_b, hi, hg = jnp.split(proj, split_at, axis=-1)
        a = neighborhood_attention(aq, ak, av, rpb[layer])
        a = group_rms_norm(a, attn_norm[layer], NA_HEADS) * jax.nn.silu(ag)
        r = hgrn2_bidirectional(hq, hf_f, hf_b, hi, lb_all[layer]).astype(h.dtype)
        r = group_rms_norm(r, hgrn_norm[layer], HG_HEADS) * jax.nn.silu(hg)
        h = h + jnp.concatenate([a, r], axis=-1) @ w_out[layer]
        gate = jax.nn.sigmoid(rms_norm(h, ln_ple[layer]) @ w_pg[layer])
        h = h + gate * (p[layer] @ w_pp[layer])
    return rms_norm(h, ln_final)
```

```python
import functools

import jax
import jax.numpy as jnp
import numpy as np
from jax import lax
from jax.experimental import pallas as pl
from jax.experimental.pallas import tpu as pltpu

F32 = jnp.float32
BF16 = jnp.bfloat16

D_MODEL = 1024
GRID_W = 64
NA_WIDTH = 512
HG_WIDTH = 512
NA_HEADS = 8
NA_HEAD_DIM = 64
NA_ROWS = 8
NA_COLS = 16
HG_HEAD_DIM = 128
HG_HEADS = 4
PLE_DIM = 256
NORM_EPS = 1e-6
N_GROUPS = 9
PROJ_TOTAL = N_GROUPS * 512

V7X_LANES = 128
V7X_VMEM_LIMIT = 56 * 1024 * 1024
MASK_VALUE = -1e30

PROJ_TM = 512
OUT_TM = 512
HG_CHUNK = 128
HG_STEP = 512
NA_PAIRS = NA_HEADS // 2


def _sigmoid(x):
    return 1.0 / (1.0 + jnp.exp(-x))


def _proj_kernel(x_ref, g_ref, w_ref, lbl_ref,
                 aq_ref, ak_ref, av_ref, ag_ref, hq_ref,
                 lff_ref, kf_ref, lfb_ref, kb_ref, hv_ref, hg_ref):
    x = x_ref[...]
    ms = jnp.mean(x * x, axis=-1, keepdims=True)
    hn = (x * lax.rsqrt(ms + NORM_EPS) * g_ref[...]).astype(BF16)

    l0 = lbl_ref[0]
    l1 = lbl_ref[1]
    m = jnp.maximum(l0, l1)
    e0 = jnp.exp(l0 - m)
    e1 = jnp.exp(l1 - m)
    lb = e0 / (e0 + e1)

    def group(j):
        return jnp.dot(hn, w_ref[:, j * 512:(j + 1) * 512], preferred_element_type=F32)

    aq_ref[...] = (group(0) * (NA_HEAD_DIM ** -0.5)).astype(BF16)
    ak_ref[...] = group(1).astype(BF16)
    av_ref[...] = group(2).astype(BF16)
    t = group(3)
    ag_ref[...] = (t * _sigmoid(t)).astype(BF16)
    t = group(4)
    hq_ref[...] = (t * _sigmoid(t)).astype(BF16)
    for j, lf_ref, k_ref in ((5, lff_ref, kf_ref), (6, lfb_ref, kb_ref)):
        lbj = lb[j - 5:j - 4, :]
        f = lbj + (1.0 - lbj) * _sigmoid(group(j))
        lf_ref[...] = jnp.log(f).astype(BF16)
        k_ref[...] = (1.0 - f).astype(BF16)
    hv_ref[...] = group(7).astype(BF16)
    t = group(8)
    hg_ref[...] = (t * _sigmoid(t)).astype(BF16)


def _proj(x2, ln_mix, w_in, lb_logits):
    n = x2.shape[0]
    out_sds = jax.ShapeDtypeStruct((n, 512), BF16)
    row_spec = pl.BlockSpec((PROJ_TM, 512), lambda i: (i, 0))
    return pl.pallas_call(
        _proj_kernel,
        grid=(n // PROJ_TM,),
        in_specs=[
            pl.BlockSpec((PROJ_TM, D_MODEL), lambda i: (i, 0)),
            pl.BlockSpec((1, D_MODEL), lambda i: (0, 0)),
            pl.BlockSpec((D_MODEL, PROJ_TOTAL), lambda i: (0, 0)),
            pl.BlockSpec((2, 2, HG_WIDTH), lambda i: (0, 0, 0)),
        ],
        out_specs=[row_spec] * 11,
        out_shape=[out_sds] * 11,
        compiler_params=pltpu.CompilerParams(
            dimension_semantics=("parallel",), vmem_limit_bytes=V7X_VMEM_LIMIT),
        name="proj",
    )(x2, ln_mix, w_in, lb_logits)


def _attn_kernel(q_ref, k_ref, v_ref, sg_ref, tab_ref, g_ref, o_ref):
    rows = q_ref.shape[1] // GRID_W
    kh = min(NA_ROWS, rows)
    nkeys = kh * GRID_W
    lane = lax.broadcasted_iota(jnp.int32, (GRID_W, V7X_LANES), 1)
    first = lane < NA_HEAD_DIM
    m0 = first.astype(F32).astype(BF16)
    m1 = (1.0 - first.astype(F32)).astype(BF16)
    gain = g_ref[...]

    def body(r, carry):
        rs = jnp.clip(r - kh // 2, 0, rows - kh)
        d = r - rs
        q0 = pl.multiple_of(r * GRID_W, GRID_W)
        k0 = pl.multiple_of(rs * GRID_W, GRID_W)
        q = q_ref[0, pl.ds(q0, GRID_W), :]
        kk = k_ref[0, pl.ds(k0, nkeys), :]
        vv = v_ref[0, pl.ds(k0, nkeys), :]
        qm = jnp.concatenate([q * m0, q * m1], axis=0)
        s = lax.dot_general(qm, kk, (((1,), (1,)), ((), ())), preferred_element_type=F32)
        s = s + tab_ref[0, d]
        mx = jnp.max(s, axis=-1, keepdims=True)
        p = jnp.exp(s - mx)
        den = jnp.sum(p, axis=-1, keepdims=True)
        o2 = jnp.dot(p.astype(BF16), vv, preferred_element_type=F32) / den
        o = jnp.where(first, o2[:GRID_W], o2[GRID_W:])
        sq = o * o
        ms0 = jnp.sum(jnp.where(first, sq, 0.0), axis=-1, keepdims=True)
        ms1 = jnp.sum(jnp.where(first, 0.0, sq), axis=-1, keepdims=True)
        ms = jnp.where(first, ms0, ms1) * (1.0 / NA_HEAD_DIM)
        y = o * lax.rsqrt(ms + NORM_EPS) * gain
        y = y * sg_ref[0, pl.ds(q0, GRID_W), :].astype(F32)
        o_ref[0, pl.ds(q0, GRID_W), :] = y.astype(BF16)
        return carry

    lax.fori_loop(0, rows, body, 0)


def _attn_bias_table(rpb_layer, rows):
    kh = min(NA_ROWS, rows)
    cols = np.arange(GRID_W)
    col_start = np.clip(cols - NA_COLS // 2, 0, GRID_W - NA_COLS)
    kc = cols[None, :]
    valid = (kc >= col_start[:, None]) & (kc < col_start[:, None] + NA_COLS)
    dc = np.clip(kc - cols[:, None] + NA_COLS - 1, 0, 2 * NA_COLS - 2)
    dr = np.arange(kh)[None, :] - np.arange(kh)[:, None] + NA_ROWS - 1
    t = rpb_layer.astype(F32)[:, dr][:, :, :, dc]
    t = jnp.where(valid[None, None, None], t, MASK_VALUE)
    t = t.transpose(0, 1, 3, 2, 4).reshape(NA_PAIRS, 2, kh, GRID_W, kh * GRID_W)
    return t.transpose(0, 2, 1, 3, 4).reshape(NA_PAIRS, kh, 2 * GRID_W, kh * GRID_W)


def _attn(aq, ak, av, ag, tab, attn_norm):
    b, s, _ = aq.shape
    kh = tab.shape[1]
    blk = pl.BlockSpec((1, s, V7X_LANES), lambda hp, i: (i, 0, hp))
    return pl.pallas_call(
        _attn_kernel,
        grid=(NA_PAIRS, b),
        in_specs=[blk, blk, blk, blk,
                  pl.BlockSpec((1, kh, 2 * GRID_W, kh * GRID_W), lambda hp, i: (hp, 0, 0, 0)),
                  pl.BlockSpec((1, V7X_LANES), lambda hp, i: (0, hp))],
        out_specs=blk,
        out_shape=jax.ShapeDtypeStruct((b, s, NA_WIDTH), BF16),
        compiler_params=pltpu.CompilerParams(
            dimension_semantics=("parallel", "parallel"), vmem_limit_bytes=V7X_VMEM_LIMIT),
        name="attn",
    )(aq, ak, av, ag, tab, attn_norm)


def _shift_rows(x, d, rev):
    c = x.shape[0]
    row = lax.broadcasted_iota(jnp.int32, x.shape, 0)
    if d % 8 == 0:
        z = jnp.zeros((d, x.shape[1]), x.dtype)
        return jnp.concatenate([x[d:], z], axis=0) if rev else jnp.concatenate([z, x[:c - d]], axis=0)
    if rev:
        return jnp.where(row < c - d, pltpu.roll(x, c - d, 0), 0.0)
    return jnp.where(row >= d, pltpu.roll(x, d, 0), 0.0)


def _cumsum_rows(x, rev):
    d = 1
    while d < x.shape[0]:
        x = x + _shift_rows(x, d, rev)
        d *= 2
    return x


def _block_ref(b, n, rev):
    c, w = b.shape
    pos = n if rev else n - 1
    if 2 * n >= 8:
        b3 = b.reshape(c // (2 * n), 2 * n, w)
        return jnp.broadcast_to(b3[:, pos:pos + 1, :], b3.shape).reshape(c, w)
    row = lax.broadcasted_iota(jnp.int32, b.shape, 0)
    if n == 1:
        if rev:
            return jnp.where(row % 2 == 0, pltpu.roll(b, c - 1, 0), b)
        return jnp.where(row % 2 == 1, pltpu.roll(b, 1, 0), b)
    assert n == 2
    if rev:
        c1 = jnp.where(row % 2 == 1, pltpu.roll(b, 1, 0), b)
        return jnp.where(row % 4 < 2, pltpu.roll(c1, c - 2, 0), c1)
    c1 = jnp.where(row % 2 == 0, pltpu.roll(b, c - 1, 0), b)
    return jnp.where(row % 4 >= 2, pltpu.roll(c1, 2, 0), c1)


def _hgrn_chunk(q, kk, v, lf, st, rev):
    c = q.shape[0]
    nt = (((1,), (1,)), ((), ()))
    b = _cumsum_rows(lf, rev)
    btot = b[0:1, :] if rev else b[c - 1:c, :]

    o = lax.dot_general((q * jnp.exp(b)).astype(BF16), st.astype(BF16), nt,
                        preferred_element_type=F32)
    o = o + jnp.sum(q * kk, axis=-1, keepdims=True) * v

    t_idx = lax.broadcasted_iota(jnp.int32, (c, c), 0)
    s_idx = lax.broadcasted_iota(jnp.int32, (c, c), 1)
    x = t_idx ^ s_idx
    causal = (t_idx < s_idx) if rev else (t_idx > s_idx)
    scores = jnp.zeros((c, c), F32)
    n = 1
    while n < c:
        e = jnp.exp(-jnp.abs(b - _block_ref(b, n, rev)))
        sl = lax.dot_general((q * e).astype(BF16), (kk * e).astype(BF16), nt,
                             preferred_element_type=F32)
        scores = jnp.where(causal & (x >= n) & (x < 2 * n), sl, scores)
        n *= 2
    o = o + jnp.dot(scores.astype(BF16), v.astype(BF16), preferred_element_type=F32)

    kd = (kk * jnp.exp(btot - b)).astype(BF16)
    upd = lax.dot_general(v.astype(BF16), kd, (((0,), (0,)), ((), ())),
                          preferred_element_type=F32)
    return o, st * jnp.exp(btot) + upd


def _hgrn_kernel(qf_ref, vf_ref, lff_ref, kf_ref, qb_ref, vb_ref, lfb_ref, kb_ref,
                 of_ref, ob_ref, st_ref):
    @pl.when(pl.program_id(1) == 0)
    def _():
        st_ref[...] = jnp.zeros_like(st_ref)

    n_sub = qf_ref.shape[1] // HG_CHUNK
    dirs = ((0, False, qf_ref, kf_ref, vf_ref, lff_ref, of_ref),
            (1, True, qb_ref, kb_ref, vb_ref, lfb_ref, ob_ref))

    def body(j, carry):
        for di, rev, q_ref, k_ref, v_ref, lf_ref, o_ref in dirs:
            sub = (n_sub - 1 - j) if rev else j
            r0 = pl.multiple_of(sub * HG_CHUNK, HG_CHUNK)
            for h in range(HG_HEADS):
                cs = slice(h * HG_HEAD_DIM, (h + 1) * HG_HEAD_DIM)
                ld = lambda ref: ref[0, pl.ds(r0, HG_CHUNK), cs].astype(F32)
                o, st = _hgrn_chunk(ld(q_ref), ld(k_ref), ld(v_ref), ld(lf_ref),
                                    st_ref[di, h], rev)
                st_ref[di, h] = st
                o_ref[0, pl.ds(r0, HG_CHUNK), cs] = o.astype(o_ref.dtype)
        return carry

    lax.fori_loop(0, n_sub, body, 0)


def _hgrn(hq, hv, lff, kf, lfb, kb):
    b, s, w = hq.shape
    ns = s // HG_STEP
    fwd = pl.BlockSpec((1, HG_STEP, w), lambda i, j: (i, j, 0))
    bwd = pl.BlockSpec((1, HG_STEP, w), lambda i, j: (i, ns - 1 - j, 0))
    sds = jax.ShapeDtypeStruct((b, s, w), BF16)
    return pl.pallas_call(
        _hgrn_kernel,
        grid=(b, ns),
        in_specs=[fwd, fwd, fwd, fwd, bwd, bwd, bwd, bwd],
        out_specs=[fwd, bwd],
        out_shape=[sds, sds],
        scratch_shapes=[pltpu.VMEM((2, HG_HEADS, HG_HEAD_DIM, HG_HEAD_DIM), F32)],
        compiler_params=pltpu.CompilerParams(
            dimension_semantics=("parallel", "arbitrary"), vmem_limit_bytes=V7X_VMEM_LIMIT),
        name="hgrn",
    )(hq, hv, lff, kf, hq, hv, lfb, kb)


def _rms(x, g):
    ms = jnp.mean(x * x, axis=-1, keepdims=True)
    return x * lax.rsqrt(ms + NORM_EPS) * g


def _out_kernel(a_ref, of_ref, ob_ref, hg_ref, x_ref, p_ref,
                gh_ref, wo_ref, gp_ref, wg_ref, wp_ref, gf_ref, o_ref):
    r = of_ref[...].astype(F32) + ob_ref[...].astype(F32)
    gh = gh_ref[...]
    parts = [a_ref[...]]
    for h in range(HG_HEADS):
        cs = slice(h * HG_HEAD_DIM, (h + 1) * HG_HEAD_DIM)
        y = _rms(r[:, cs], gh[:, cs]) * hg_ref[:, cs].astype(F32)
        parts.append(y.astype(BF16))
    cat = jnp.concatenate(parts, axis=-1)
    h1 = x_ref[...] + jnp.dot(cat, wo_ref[...], preferred_element_type=F32)
    gate = _sigmoid(jnp.dot(_rms(h1, gp_ref[...]).astype(BF16), wg_ref[...],
                            preferred_element_type=F32))
    pp = jnp.dot(p_ref[...].astype(BF16), wp_ref[...], preferred_element_type=F32)
    o_ref[...] = _rms(h1 + gate * pp, gf_ref[...])


def _out(a, o_f, o_b, hg, x2, p2, hgrn_norm, w_out, ln_ple, w_pg, w_pp, ln_final):
    n = x2.shape[0]
    row = lambda w: pl.BlockSpec((OUT_TM, w), lambda i: (i, 0))
    full = lambda r, c: pl.BlockSpec((r, c), lambda i: (0, 0))
    return pl.pallas_call(
        _out_kernel,
        grid=(n // OUT_TM,),
        in_specs=[row(NA_WIDTH), row(HG_WIDTH), row(HG_WIDTH), row(HG_WIDTH),
                  row(D_MODEL), row(PLE_DIM),
                  full(1, HG_WIDTH), full(D_MODEL, D_MODEL), full(1, D_MODEL),
                  full(D_MODEL, D_MODEL), full(PLE_DIM, D_MODEL), full(1, D_MODEL)],
        out_specs=row(D_MODEL),
        out_shape=jax.ShapeDtypeStruct((n, D_MODEL), F32),
        compiler_params=pltpu.CompilerParams(
            dimension_semantics=("parallel",), vmem_limit_bytes=V7X_VMEM_LIMIT),
        name="out",
    )(a, o_f, o_b, hg, x2, p2, hgrn_norm, w_out, ln_ple, w_pg, w_pp, ln_final)


def kernel(x, p, ln_mix, w_in, rpb, lb_logits, attn_norm, hgrn_norm, w_out, ln_ple, w_pg, w_pp, ln_final):
    depth = p.shape[0]
    assert depth == 1 and lb_logits.shape[0] == 2
    b, s, d = x.shape
    n = b * s
    rows = s // GRID_W
    assert d == D_MODEL and n % PROJ_TM == 0 and s % HG_STEP == 0 and rows >= NA_ROWS

    x2 = x.reshape(n, d)
    (aq, ak, av, ag, hq, lff, kf, lfb, kb, hv, hg) = _proj(
        x2, ln_mix[0:1], w_in[0].astype(BF16), lb_logits)

    r3 = lambda t: t.reshape(b, s, t.shape[-1])
    tab = _attn_bias_table(rpb[0], rows)
    a = _attn(r3(aq), r3(ak), r3(av), r3(ag), tab, attn_norm[0:1])
    o_f, o_b = _hgrn(r3(hq), r3(hv), r3(lff), r3(kf), r3(lfb), r3(kb))

    out = _out(a.reshape(n, NA_WIDTH), o_f.reshape(n, HG_WIDTH), o_b.reshape(n, HG_WIDTH), hg,
               x2, p[0].reshape(n, PLE_DIM), hgrn_norm[0:1], w_out[0].astype(BF16),
               ln_ple[0:1], w_pg[0].astype(BF16), w_pp[0].astype(BF16), ln_final.reshape(1, d))
    return out.reshape(b, s, d)
```

```python
import functools

import jax
import jax.numpy as jnp
import numpy as np
from jax import lax
from jax.experimental import pallas as pl
from jax.experimental.pallas import tpu as pltpu

F32 = jnp.float32
BF16 = jnp.bfloat16

D_MODEL = 1024
GRID_W = 64
NA_WIDTH = 512
HG_WIDTH = 512
NA_HEADS = 8
NA_HEAD_DIM = 64
NA_ROWS = 8
NA_COLS = 16
HG_HEAD_DIM = 128
HG_HEADS = 4
PLE_DIM = 256
NORM_EPS = 1e-6
N_GROUPS = 9
PROJ_TOTAL = N_GROUPS * 512

V7X_LANES = 128
V7X_VMEM_LIMIT = 56 * 1024 * 1024
MASK_VALUE = -1e30
LOG2E = 1.4426950408889634

PROJ_TM = 512
OUT_TM = 512
HG_CHUNK = 128
HG_STEP = 512
HG_BASE = HG_CHUNK
HG_MAX_LOG2_RANGE = 110.0
NA_PAIRS = NA_HEADS // 2
NA_ROWS_PER_ITER = 8


def _sigmoid(x):
    return 1.0 / (1.0 + jnp.exp(-x))


def _proj_kernel(x_ref, g_ref, w_ref, lbl_ref,
                 aq_ref, ak_ref, av_ref, ag_ref, hq_ref,
                 lff_ref, kf_ref, lfb_ref, kb_ref, hv_ref, hg_ref):
    x = x_ref[...]
    ms = jnp.mean(x * x, axis=-1, keepdims=True)
    hn = (x * lax.rsqrt(ms + NORM_EPS) * g_ref[...]).astype(BF16)

    l0 = lbl_ref[0]
    l1 = lbl_ref[1]
    m = jnp.maximum(l0, l1)
    e0 = jnp.exp(l0 - m)
    e1 = jnp.exp(l1 - m)
    lb = e0 / (e0 + e1)

    def group(j):
        return jnp.dot(hn, w_ref[:, j * 512:(j + 1) * 512], preferred_element_type=F32)

    aq_ref[...] = (group(0) * (NA_HEAD_DIM ** -0.5)).astype(BF16)
    ak_ref[...] = group(1).astype(BF16)
    av_ref[...] = group(2).astype(BF16)
    t = group(3)
    ag_ref[...] = (t * _sigmoid(t)).astype(BF16)
    t = group(4)
    hq_ref[...] = (t * _sigmoid(t)).astype(BF16)
    for j, lf_ref, k_ref in ((5, lff_ref, kf_ref), (6, lfb_ref, kb_ref)):
        lbj = lb[j - 5:j - 4, :]
        f = lbj + (1.0 - lbj) * _sigmoid(group(j))
        lf_ref[...] = jnp.log(f).astype(BF16)
        k_ref[...] = (1.0 - f).astype(BF16)
    hv_ref[...] = group(7).astype(BF16)
    t = group(8)
    hg_ref[...] = (t * _sigmoid(t)).astype(BF16)


def _proj(x2, ln_mix, w_in, lb_logits):
    n = x2.shape[0]
    out_sds = jax.ShapeDtypeStruct((n, 512), BF16)
    row_spec = pl.BlockSpec((PROJ_TM, 512), lambda i: (i, 0))
    return pl.pallas_call(
        _proj_kernel,
        grid=(n // PROJ_TM,),
        in_specs=[
            pl.BlockSpec((PROJ_TM, D_MODEL), lambda i: (i, 0)),
            pl.BlockSpec((1, D_MODEL), lambda i: (0, 0)),
            pl.BlockSpec((D_MODEL, PROJ_TOTAL), lambda i: (0, 0)),
            pl.BlockSpec((2, 2, HG_WIDTH), lambda i: (0, 0, 0)),
        ],
        out_specs=[row_spec] * 11,
        out_shape=[out_sds] * 11,
        compiler_params=pltpu.CompilerParams(
            dimension_semantics=("parallel",), vmem_limit_bytes=V7X_VMEM_LIMIT),
        name="proj",
    )(x2, ln_mix, w_in, lb_logits)


def _attn_kernel(q_ref, k_ref, v_ref, sg_ref, tab_ref, g_ref, o_ref):
    rows = q_ref.shape[1] // GRID_W
    kh = min(NA_ROWS, rows)
    nkeys = kh * GRID_W
    lane = lax.broadcasted_iota(jnp.int32, (GRID_W, V7X_LANES), 1)
    first = lane < NA_HEAD_DIM
    m0 = first.astype(F32).astype(BF16)
    m1 = (1.0 - first.astype(F32)).astype(BF16)
    gain = g_ref[...]

    def scores(r):
        rs = jnp.clip(r - kh // 2, 0, rows - kh)
        q0 = pl.multiple_of(r * GRID_W, GRID_W)
        k0 = pl.multiple_of(rs * GRID_W, GRID_W)
        q = q_ref[0, pl.ds(q0, GRID_W), :]
        kk = k_ref[0, pl.ds(k0, nkeys), :]
        qm = jnp.concatenate([q * m0, q * m1], axis=0)
        s = lax.dot_general(qm, kk, (((1,), (1,)), ((), ())), preferred_element_type=F32)
        return q0, k0, s + tab_ref[0, r - rs]

    def softmax(s):
        p = jnp.exp(s - jnp.max(s, axis=-1, keepdims=True))
        return p.astype(BF16), jnp.sum(p, axis=-1, keepdims=True)

    def finish(q0, o2, den):
        o2 = o2 / den
        o = jnp.where(first, o2[:GRID_W], o2[GRID_W:])
        sq = o * o
        ms0 = jnp.sum(jnp.where(first, sq, 0.0), axis=-1, keepdims=True)
        ms1 = jnp.sum(jnp.where(first, 0.0, sq), axis=-1, keepdims=True)
        ms = jnp.where(first, ms0, ms1) * (1.0 / NA_HEAD_DIM)
        y = o * lax.rsqrt(ms + NORM_EPS) * gain
        y = y * sg_ref[0, pl.ds(q0, GRID_W), :].astype(F32)
        o_ref[0, pl.ds(q0, GRID_W), :] = y.astype(BF16)

    def body(i, carry):
        rs_ = [scores(i * NA_ROWS_PER_ITER + u) for u in range(NA_ROWS_PER_ITER)]
        ps = [softmax(s) for _, _, s in rs_]
        os_ = [jnp.dot(p, v_ref[0, pl.ds(k0, nkeys), :], preferred_element_type=F32)
               for (_, k0, _), (p, _) in zip(rs_, ps)]
        for (q0, _, _), (_, den), o2 in zip(rs_, ps, os_):
            finish(q0, o2, den)
        return carry

    lax.fori_loop(0, rows // NA_ROWS_PER_ITER, body, 0)


def _attn_bias_table(rpb_layer, rows):
    kh = min(NA_ROWS, rows)
    cols = np.arange(GRID_W)
    col_start = np.clip(cols - NA_COLS // 2, 0, GRID_W - NA_COLS)
    kc = cols[None, :]
    valid = (kc >= col_start[:, None]) & (kc < col_start[:, None] + NA_COLS)
    dc = np.clip(kc - cols[:, None] + NA_COLS - 1, 0, 2 * NA_COLS - 2)
    dr = np.arange(kh)[None, :] - np.arange(kh)[:, None] + NA_ROWS - 1
    t = rpb_layer.astype(F32)[:, dr][:, :, :, dc]
    t = jnp.where(valid[None, None, None], t, MASK_VALUE)
    t = t.transpose(0, 1, 3, 2, 4).reshape(NA_PAIRS, 2, kh, GRID_W, kh * GRID_W)
    return t.transpose(0, 2, 1, 3, 4).reshape(NA_PAIRS, kh, 2 * GRID_W, kh * GRID_W)


def _attn(aq, ak, av, ag, tab, attn_norm):
    b, s, _ = aq.shape
    kh = tab.shape[1]
    blk = pl.BlockSpec((1, s, V7X_LANES), lambda hp, i: (i, 0, hp))
    return pl.pallas_call(
        _attn_kernel,
        grid=(NA_PAIRS, b),
        in_specs=[blk, blk, blk, blk,
                  pl.BlockSpec((1, kh, 2 * GRID_W, kh * GRID_W), lambda hp, i: (hp, 0, 0, 0)),
                  pl.BlockSpec((1, V7X_LANES), lambda hp, i: (0, hp))],
        out_specs=blk,
        out_shape=jax.ShapeDtypeStruct((b, s, NA_WIDTH), BF16),
        compiler_params=pltpu.CompilerParams(
            dimension_semantics=("parallel", "parallel"), vmem_limit_bytes=V7X_VMEM_LIMIT),
        name="attn",
    )(aq, ak, av, ag, tab, attn_norm)


def _block_ref(b, n, rev):
    c, w = b.shape
    pos = n if rev else n - 1
    if 2 * n >= 8:
        b3 = b.reshape(c // (2 * n), 2 * n, w)
        return jnp.broadcast_to(b3[:, pos:pos + 1, :], b3.shape).reshape(c, w)
    row = lax.broadcasted_iota(jnp.int32, b.shape, 0)
    if n == 1:
        if rev:
            return jnp.where(row % 2 == 0, pltpu.roll(b, c - 1, 0), b)
        return jnp.where(row % 2 == 1, pltpu.roll(b, 1, 0), b)
    assert n == 2
    if rev:
        c1 = jnp.where(row % 2 == 1, pltpu.roll(b, 1, 0), b)
        return jnp.where(row % 4 < 2, pltpu.roll(c1, c - 2, 0), c1)
    c1 = jnp.where(row % 2 == 0, pltpu.roll(b, c - 1, 0), b)
    return jnp.where(row % 4 >= 2, pltpu.roll(c1, 2, 0), c1)


def _level_decay(b, n, rev):
    c, w = b.shape
    if n < 8:
        return jnp.exp2(-jnp.abs(b - _block_ref(b, n, rev)))
    b3 = b.reshape(c // (2 * n), 2 * n, w)
    lo, hi = b3[:, :n], b3[:, n:]
    if rev:
        ref = b3[:, n:n + 1]
        parts = [jnp.exp2(lo - ref), jnp.exp2(ref - hi)]
    else:
        ref = b3[:, n - 1:n]
        parts = [jnp.exp2(ref - lo), jnp.exp2(hi - ref)]
    return jnp.concatenate(parts, axis=1).reshape(c, w)


def _base_ref(b3, rev):
    base = b3.shape[1]
    pos = base // 2 if rev else base // 2 - 1
    return b3[:, pos:pos + 1]


def _level_ids(c, rev, base):
    t_idx = lax.broadcasted_iota(jnp.int32, (c, c), 0)
    s_idx = lax.broadcasted_iota(jnp.int32, (c, c), 1)
    x = t_idx ^ s_idx
    lev = jnp.ones((c, c), jnp.int32)
    n = 2 * base
    while n < c:
        lev = lev + (x >= n).astype(jnp.int32)
        n *= 2
    before = (t_idx < s_idx) if rev else (t_idx > s_idx)
    lev = jnp.where(before & (x >= base), lev, -1)
    return jnp.where((x < base) & (before | (t_idx == s_idx)), 0, lev)


def _hgrn_scores(q, kk, b, lev, rev, base):
    c, w = q.shape
    nt = (((1,), (1,)), ((), ()))

    if base == 1:
        scores = lax.dot_general(q, kk, nt, preferred_element_type=F32)
    else:
        b3 = b.reshape(c // base, base, w)
        d = (b3 - _base_ref(b3, rev)).reshape(c, w)
        scores = lax.dot_general(jnp.exp2(d).astype(BF16) * q, jnp.exp2(-d).astype(BF16) * kk, nt,
                                 preferred_element_type=F32)
    scores = jnp.where(lev == 0, scores, 0.0)
    n, li = base, 1
    while n < c:
        e = _level_decay(b, n, rev).astype(BF16)
        sl = lax.dot_general(e * q, e * kk, nt, preferred_element_type=F32)
        scores = jnp.where(lev == li, sl, scores)
        n, li = 2 * n, li + 1
    return scores.astype(BF16)


def _hgrn_output(scores, q, kk, v, b, st, rev):
    c, w = q.shape
    btot = b[0:1, :] if rev else b[c - 1:c, :]
    lhs = jnp.concatenate([scores, jnp.exp2(b).astype(BF16) * q], axis=1)
    rhs = jnp.concatenate([v, st.astype(BF16)], axis=0)
    o = jnp.dot(lhs, rhs, preferred_element_type=F32)

    kd = jnp.exp2(btot - b).astype(BF16) * kk
    upd = lax.dot_general(kd, v, (((0,), (0,)), ((), ())), preferred_element_type=F32)
    decay = jnp.broadcast_to(jnp.exp2(btot), (w, w)).T
    return o, st * decay + upd


def _hgrn_kernel(qf_ref, vf_ref, lff_ref, kf_ref, qb_ref, vb_ref, lfb_ref, kb_ref,
                 of_ref, ob_ref, st_ref, b_ref):
    @pl.when(pl.program_id(1) == 0)
    def _():
        st_ref[...] = jnp.zeros_like(st_ref)

    c = HG_CHUNK
    n_sub = qf_ref.shape[1] // c
    t_idx = lax.broadcasted_iota(jnp.int32, (c, c), 0)
    u_idx = lax.broadcasted_iota(jnp.int32, (c, c), 1)
    dirs = ((0, False, qf_ref, kf_ref, vf_ref, lff_ref, of_ref),
            (1, True, qb_ref, kb_ref, vb_ref, lfb_ref, ob_ref))

    widest = jnp.zeros((c // HG_BASE, 1, lff_ref.shape[2]), F32)
    for di, rev, _, _, _, lf_ref, _ in dirs:
        tri = ((u_idx >= t_idx) if rev else (u_idx <= t_idx)).astype(F32).astype(BF16)
        for j in range(n_sub):
            rows = slice(j * c, (j + 1) * c)
            bj = jnp.dot(tri, lf_ref[0, rows, :], preferred_element_type=F32) * LOG2E
            b_ref[di, rows, :] = bj
            b3 = bj.reshape(c // HG_BASE, HG_BASE, bj.shape[1])
            ref = _base_ref(b3, rev)
            widest = jnp.maximum(widest, jnp.maximum(jnp.abs(b3[:, 0:1] - ref),
                                                     jnp.abs(b3[:, HG_BASE - 1:HG_BASE] - ref)))
    fast = jnp.max(widest) <= HG_MAX_LOG2_RANGE

    def run(base):
        lev = {rev: _level_ids(c, rev, base) for rev in (False, True)}

        def body(j, carry):
            work = []
            for di, rev, q_ref, k_ref, v_ref, _, o_ref in dirs:
                r0 = pl.multiple_of(((n_sub - 1 - j) if rev else j) * c, c)
                for h in range(HG_HEADS):
                    cs = slice(h * HG_HEAD_DIM, (h + 1) * HG_HEAD_DIM)
                    ld = lambda ref: ref[0, pl.ds(r0, c), cs]
                    q, kk, bb = ld(q_ref), ld(k_ref), b_ref[di, pl.ds(r0, c), cs]
                    scores = _hgrn_scores(q, kk, bb, lev[rev], rev, base)
                    work.append((di, rev, h, r0, cs, q, kk, ld(v_ref), bb, scores, o_ref))
            for di, rev, h, r0, cs, q, kk, v, bb, scores, o_ref in work:
                o, st = _hgrn_output(scores, q, kk, v, bb, st_ref[di, h], rev)
                st_ref[di, h] = st
                o_ref[0, pl.ds(r0, c), cs] = o.astype(o_ref.dtype)
            return carry

        lax.fori_loop(0, n_sub, body, 0)

    pl.when(fast)(lambda: run(HG_BASE))
    pl.when(jnp.logical_not(fast))(lambda: run(1))


def _hgrn(hq, hv, lff, kf, lfb, kb):
    b, s, w = hq.shape
    ns = s // HG_STEP
    fwd = pl.BlockSpec((1, HG_STEP, w), lambda i, j: (i, j, 0))
    bwd = pl.BlockSpec((1, HG_STEP, w), lambda i, j: (i, ns - 1 - j, 0))
    sds = jax.ShapeDtypeStruct((b, s, w), BF16)
    return pl.pallas_call(
        _hgrn_kernel,
        grid=(b, ns),
        in_specs=[fwd, fwd, fwd, fwd, bwd, bwd, bwd, bwd],
        out_specs=[fwd, bwd],
        out_shape=[sds, sds],
        scratch_shapes=[pltpu.VMEM((2, HG_HEADS, HG_HEAD_DIM, HG_HEAD_DIM), F32),
                        pltpu.VMEM((2, HG_STEP, w), F32)],
        compiler_params=pltpu.CompilerParams(
            dimension_semantics=("parallel", "arbitrary"), vmem_limit_bytes=V7X_VMEM_LIMIT),
        name="hgrn",
    )(hq, hv, lff, kf, hq, hv, lfb, kb)


def _rms(x, g):
    ms = jnp.mean(x * x, axis=-1, keepdims=True)
    return x * lax.rsqrt(ms + NORM_EPS) * g


def _out_kernel(a_ref, of_ref, ob_ref, hg_ref, x_ref, p_ref,
                gh_ref, wo_ref, gp_ref, wg_ref, wp_ref, gf_ref, o_ref):
    r = of_ref[...].astype(F32) + ob_ref[...].astype(F32)
    gh = gh_ref[...]
    parts = [a_ref[...]]
    for h in range(HG_HEADS):
        cs = slice(h * HG_HEAD_DIM, (h + 1) * HG_HEAD_DIM)
        y = _rms(r[:, cs], gh[:, cs]) * hg_ref[:, cs].astype(F32)
        parts.append(y.astype(BF16))
    cat = jnp.concatenate(parts, axis=-1)
    h1 = x_ref[...] + jnp.dot(cat, wo_ref[...], preferred_element_type=F32)
    gate = _sigmoid(jnp.dot(_rms(h1, gp_ref[...]).astype(BF16), wg_ref[...],
                            preferred_element_type=F32))
    pp = jnp.dot(p_ref[...].astype(BF16), wp_ref[...], preferred_element_type=F32)
    o_ref[...] = _rms(h1 + gate * pp, gf_ref[...])


def _out(a, o_f, o_b, hg, x2, p2, hgrn_norm, w_out, ln_ple, w_pg, w_pp, ln_final):
    n = x2.shape[0]
    row = lambda w: pl.BlockSpec((OUT_TM, w), lambda i: (i, 0))
    full = lambda r, c: pl.BlockSpec((r, c), lambda i: (0, 0))
    return pl.pallas_call(
        _out_kernel,
        grid=(n // OUT_TM,),
        in_specs=[row(NA_WIDTH), row(HG_WIDTH), row(HG_WIDTH), row(HG_WIDTH),
                  row(D_MODEL), row(PLE_DIM),
                  full(1, HG_WIDTH), full(D_MODEL, D_MODEL), full(1, D_MODEL),
                  full(D_MODEL, D_MODEL), full(PLE_DIM, D_MODEL), full(1, D_MODEL)],
        out_specs=row(D_MODEL),
        out_shape=jax.ShapeDtypeStruct((n, D_MODEL), F32),
        compiler_params=pltpu.CompilerParams(
            dimension_semantics=("parallel",), vmem_limit_bytes=V7X_VMEM_LIMIT),
        name="out",
    )(a, o_f, o_b, hg, x2, p2, hgrn_norm, w_out, ln_ple, w_pg, w_pp, ln_final)


def kernel(x, p, ln_mix, w_in, rpb, lb_logits, attn_norm, hgrn_norm, w_out, ln_ple, w_pg, w_pp, ln_final):
    depth = p.shape[0]
    assert depth == 1 and lb_logits.shape[0] == 2
    b, s, d = x.shape
    n = b * s
    rows = s // GRID_W
    assert d == D_MODEL and n % PROJ_TM == 0 and s % HG_STEP == 0 and rows >= NA_ROWS

    x2 = x.reshape(n, d)
    (aq, ak, av, ag, hq, lff, kf, lfb, kb, hv, hg) = _proj(
        x2, ln_mix[0:1], w_in[0].astype(BF16), lb_logits)

    r3 = lambda t: t.reshape(b, s, t.shape[-1])
    tab = _attn_bias_table(rpb[0], rows)
    a = _attn(r3(aq), r3(ak), r3(av), r3(ag), tab, attn_norm[0:1])
    o_f, o_b = _hgrn(r3(hq), r3(hv), r3(lff), r3(kf), r3(lfb), r3(kb))

    out = _out(a.reshape(n, NA_WIDTH), o_f.reshape(n, HG_WIDTH), o_b.reshape(n, HG_WIDTH), hg,
               x2, p[0].reshape(n, PLE_DIM), hgrn_norm[0:1], w_out[0].astype(BF16),
               ln_ple[0:1], w_pg[0].astype(BF16), w_pp[0].astype(BF16), ln_final.reshape(1, d))
    return out.reshape(b, s, d)
```

```python
import functools

import jax
import jax.numpy as jnp
import numpy as np
from jax import lax
from jax.experimental import pallas as pl
from jax.experimental.pallas import tpu as pltpu

F32 = jnp.float32
BF16 = jnp.bfloat16

D_MODEL = 1024
GRID_W = 64
NA_WIDTH = 512
HG_WIDTH = 512
NA_HEADS = 8
NA_HEAD_DIM = 64
NA_ROWS = 8
NA_COLS = 16
HG_HEAD_DIM = 128
HG_HEADS = 4
PLE_DIM = 256
NORM_EPS = 1e-6
N_GROUPS = 9
PROJ_TOTAL = N_GROUPS * 512

V7X_LANES = 128
V7X_VMEM_LIMIT = 56 * 1024 * 1024
MASK_VALUE = -1e30
LOG2E = 1.4426950408889634

PROJ_TM = 1024
PROJ_SPLIT = 2
OUT_TM = 1024
OUT_SPLIT = 2
HG_CHUNK = 128
HG_STEP = 512
HG_BASE = HG_CHUNK
HG_MAX_LOG2_RANGE = 110.0
NA_PAIRS = NA_HEADS // 2
NA_ROWS_PER_ITER = 8


def _sigmoid(x):
    return 1.0 / (1.0 + jnp.exp(-x))


def _proj_kernel(x_ref, g_ref, w_ref, lbl_ref,
                 aq_ref, ak_ref, av_ref, ag_ref, hq_ref,
                 lff_ref, kf_ref, lfb_ref, kb_ref, hv_ref, hg_ref):
    l0 = lbl_ref[0]
    l1 = lbl_ref[1]
    m = jnp.maximum(l0, l1)
    e0 = jnp.exp(l0 - m)
    e1 = jnp.exp(l1 - m)
    lb = e0 / (e0 + e1)

    sub = x_ref.shape[0] // PROJ_SPLIT
    tiles = [slice(u * sub, (u + 1) * sub) for u in range(PROJ_SPLIT)]
    hns = []
    for rows in tiles:
        x = x_ref[rows, :]
        ms = jnp.mean(x * x, axis=-1, keepdims=True)
        hns.append((x * lax.rsqrt(ms + NORM_EPS) * g_ref[...]).astype(BF16))

    def silu(t):
        return t * _sigmoid(t)

    def forget(lbj):
        def act(t):
            f = lbj + (1.0 - lbj) * _sigmoid(t)
            return jnp.log(f), 1.0 - f
        return act

    scale_q = lambda t: t * (NA_HEAD_DIM ** -0.5 * LOG2E)
    ident = lambda t: t
    groups = ((scale_q, (aq_ref,)), (ident, (ak_ref,)), (ident, (av_ref,)), (silu, (ag_ref,)),
              (silu, (hq_ref,)), (forget(lb[0:1, :]), (lff_ref, kf_ref)),
              (forget(lb[1:2, :]), (lfb_ref, kb_ref)), (ident, (hv_ref,)), (silu, (hg_ref,)))
    for j, (act, out_refs) in enumerate(groups):
        w = w_ref[:, j * 512:(j + 1) * 512]
        for rows, hn in zip(tiles, hns):
            res = act(jnp.dot(hn, w, preferred_element_type=F32))
            for ref, val in zip(out_refs, res if isinstance(res, tuple) else (res,)):
                ref[rows, :] = val.astype(BF16)


def _proj(x2, ln_mix, w_in, lb_logits):
    n = x2.shape[0]
    out_sds = jax.ShapeDtypeStruct((n, 512), BF16)
    row_spec = pl.BlockSpec((PROJ_TM, 512), lambda i: (i, 0))
    return pl.pallas_call(
        _proj_kernel,
        grid=(n // PROJ_TM,),
        in_specs=[
            pl.BlockSpec((PROJ_TM, D_MODEL), lambda i: (i, 0)),
            pl.BlockSpec((1, D_MODEL), lambda i: (0, 0)),
            pl.BlockSpec((D_MODEL, PROJ_TOTAL), lambda i: (0, 0), pipeline_mode=pl.Buffered(1)),
            pl.BlockSpec((2, 2, HG_WIDTH), lambda i: (0, 0, 0)),
        ],
        out_specs=[row_spec] * 11,
        out_shape=[out_sds] * 11,
        compiler_params=pltpu.CompilerParams(
            dimension_semantics=("parallel",), vmem_limit_bytes=V7X_VMEM_LIMIT),
        name="proj",
    )(x2, ln_mix, w_in, lb_logits)


def _attn_build_table(rpb_ref, tab_ref, pair):
    kh = tab_ref.shape[0]
    n_dr, n_dc = 2 * NA_ROWS - 1, 2 * NA_COLS - 1
    c_idx = lax.broadcasted_iota(jnp.int32, (GRID_W, V7X_LANES), 0)
    kc_idx = lax.broadcasted_iota(jnp.int32, (GRID_W, V7X_LANES), 1) % GRID_W
    start = jnp.clip(c_idx - NA_COLS // 2, 0, GRID_W - NA_COLS)
    valid = (kc_idx >= start) & (kc_idx < start + NA_COLS)
    diff = kc_idx - c_idx + NA_COLS - 1
    low_half = lax.broadcasted_iota(jnp.int32, (GRID_W, V7X_LANES), 1) < GRID_W
    for j in range(2):
        head = 2 * pair + j
        toeplitz = {}
        for dr in range(NA_ROWS - kh, NA_ROWS - 1 + kh):
            t = jnp.full((GRID_W, V7X_LANES), MASK_VALUE, F32)
            for dc in range(n_dc):
                t = jnp.where(diff == dc, rpb_ref[(head * n_dr + dr) * n_dc + dc] * LOG2E, t)
            toeplitz[dr] = jnp.where(valid, t, MASK_VALUE)
        for d in range(kh):
            for i in range(0, kh, 2):
                blk = jnp.where(low_half, toeplitz[i - d + NA_ROWS - 1], toeplitz[i + 1 - d + NA_ROWS - 1])
                tab_ref[d, j * GRID_W:(j + 1) * GRID_W, i * GRID_W:(i + 2) * GRID_W] = blk


def _attn_kernel(rpb_ref, q_ref, k_ref, v_ref, sg_ref, g_ref, o_ref, tab_ref):
    rows = q_ref.shape[1] // GRID_W
    kh = min(NA_ROWS, rows)
    nkeys = kh * GRID_W
    lane = lax.broadcasted_iota(jnp.int32, (GRID_W, V7X_LANES), 1)
    first = lane < NA_HEAD_DIM
    m0 = first.astype(F32).astype(BF16)
    m1 = (1.0 - first.astype(F32)).astype(BF16)
    gain = g_ref[...]

    @pl.when(pl.program_id(1) == 0)
    def _():
        _attn_build_table(rpb_ref, tab_ref, pl.program_id(0))

    def scores(r):
        rs = jnp.clip(r - kh // 2, 0, rows - kh)
        q0 = pl.multiple_of(r * GRID_W, GRID_W)
        k0 = pl.multiple_of(rs * GRID_W, GRID_W)
        q = q_ref[0, pl.ds(q0, GRID_W), :]
        kk = k_ref[0, pl.ds(k0, nkeys), :]
        qm = jnp.concatenate([q * m0, q * m1], axis=0)
        s = lax.dot_general(qm, kk, (((1,), (1,)), ((), ())), preferred_element_type=F32)
        return q0, k0, s + tab_ref[r - rs]

    def softmax(s):
        p = jnp.exp2(s - jnp.max(s, axis=-1, keepdims=True))
        return p.astype(BF16), jnp.sum(p, axis=-1, keepdims=True)

    def finish(q0, o2, den):
        o2 = o2 / den
        o = jnp.where(first, o2[:GRID_W], o2[GRID_W:])
        sq = o * o
        ms0 = jnp.sum(jnp.where(first, sq, 0.0), axis=-1, keepdims=True)
        ms1 = jnp.sum(jnp.where(first, 0.0, sq), axis=-1, keepdims=True)
        ms = jnp.where(first, ms0, ms1) * (1.0 / NA_HEAD_DIM)
        y = o * lax.rsqrt(ms + NORM_EPS) * gain
        y = y * sg_ref[0, pl.ds(q0, GRID_W), :].astype(F32)
        o_ref[0, pl.ds(q0, GRID_W), :] = y.astype(BF16)

    def body(i, carry):
        rs_ = [scores(i * NA_ROWS_PER_ITER + u) for u in range(NA_ROWS_PER_ITER)]
        ps = [softmax(s) for _, _, s in rs_]
        os_ = [jnp.dot(p, v_ref[0, pl.ds(k0, nkeys), :], preferred_element_type=F32)
               for (_, k0, _), (p, _) in zip(rs_, ps)]
        for (q0, _, _), (_, den), o2 in zip(rs_, ps, os_):
            finish(q0, o2, den)
        return carry

    lax.fori_loop(0, rows // NA_ROWS_PER_ITER, body, 0)


def _attn(rpb_flat, aq, ak, av, ag, attn_norm):
    b, s, _ = aq.shape
    kh = min(NA_ROWS, s // GRID_W)
    assert kh % 2 == 0
    blk = pl.BlockSpec((1, s, V7X_LANES), lambda hp, i: (i, 0, hp))
    return pl.pallas_call(
        _attn_kernel,
        grid=(NA_PAIRS, b),
        in_specs=[pl.BlockSpec(memory_space=pltpu.SMEM), blk, blk, blk, blk,
                  pl.BlockSpec((1, V7X_LANES), lambda hp, i: (0, hp))],
        out_specs=blk,
        out_shape=jax.ShapeDtypeStruct((b, s, NA_WIDTH), BF16),
        scratch_shapes=[pltpu.VMEM((kh, 2 * GRID_W, kh * GRID_W), F32)],
        compiler_params=pltpu.CompilerParams(
            dimension_semantics=("parallel", "arbitrary"), vmem_limit_bytes=V7X_VMEM_LIMIT),
        name="attn",
    )(rpb_flat, aq, ak, av, ag, attn_norm)


def _block_ref(b, n, rev):
    c, w = b.shape
    pos = n if rev else n - 1
    if 2 * n >= 8:
        b3 = b.reshape(c // (2 * n), 2 * n, w)
        return jnp.broadcast_to(b3[:, pos:pos + 1, :], b3.shape).reshape(c, w)
    row = lax.broadcasted_iota(jnp.int32, b.shape, 0)
    if n == 1:
        if rev:
            return jnp.where(row % 2 == 0, pltpu.roll(b, c - 1, 0), b)
        return jnp.where(row % 2 == 1, pltpu.roll(b, 1, 0), b)
    assert n == 2
    if rev:
        c1 = jnp.where(row % 2 == 1, pltpu.roll(b, 1, 0), b)
        return jnp.where(row % 4 < 2, pltpu.roll(c1, c - 2, 0), c1)
    c1 = jnp.where(row % 2 == 0, pltpu.roll(b, c - 1, 0), b)
    return jnp.where(row % 4 >= 2, pltpu.roll(c1, 2, 0), c1)


def _level_decay(b, n, rev):
    c, w = b.shape
    if n < 8:
        return jnp.exp2(-jnp.abs(b - _block_ref(b, n, rev)))
    b3 = b.reshape(c // (2 * n), 2 * n, w)
    lo, hi = b3[:, :n], b3[:, n:]
    if rev:
        ref = b3[:, n:n + 1]
        parts = [jnp.exp2(lo - ref), jnp.exp2(ref - hi)]
    else:
        ref = b3[:, n - 1:n]
        parts = [jnp.exp2(ref - lo), jnp.exp2(hi - ref)]
    return jnp.concatenate(parts, axis=1).reshape(c, w)


def _base_ref(b3, rev):
    base = b3.shape[1]
    pos = base // 2 if rev else base // 2 - 1
    return b3[:, pos:pos + 1]


def _level_ids(c, rev, base):
    t_idx = lax.broadcasted_iota(jnp.int32, (c, c), 0)
    s_idx = lax.broadcasted_iota(jnp.int32, (c, c), 1)
    x = t_idx ^ s_idx
    lev = jnp.ones((c, c), jnp.int32)
    n = 2 * base
    while n < c:
        lev = lev + (x >= n).astype(jnp.int32)
        n *= 2
    before = (t_idx < s_idx) if rev else (t_idx > s_idx)
    lev = jnp.where(before & (x >= base), lev, -1)
    return jnp.where((x < base) & (before | (t_idx == s_idx)), 0, lev)


def _hgrn_scores(q, kk, b, lev, rev, base):
    c, w = q.shape
    nt = (((1,), (1,)), ((), ()))

    if base == 1:
        scores = lax.dot_general(q, kk, nt, preferred_element_type=F32)
    else:
        b3 = b.reshape(c // base, base, w)
        d = (b3 - _base_ref(b3, rev)).reshape(c, w)
        scores = lax.dot_general(jnp.exp2(d).astype(BF16) * q, jnp.exp2(-d).astype(BF16) * kk, nt,
                                 preferred_element_type=F32)
    scores = jnp.where(lev == 0, scores, 0.0)
    n, li = base, 1
    while n < c:
        e = _level_decay(b, n, rev).astype(BF16)
        sl = lax.dot_general(e * q, e * kk, nt, preferred_element_type=F32)
        scores = jnp.where(lev == li, sl, scores)
        n, li = 2 * n, li + 1
    return scores.astype(BF16)


def _hgrn_output(scores, q, kk, v, b, st, rev):
    c, w = q.shape
    btot = b[0:1, :] if rev else b[c - 1:c, :]
    lhs = jnp.concatenate([scores, jnp.exp2(b).astype(BF16) * q], axis=1)
    rhs = jnp.concatenate([v, st.astype(BF16)], axis=0)
    o = jnp.dot(lhs, rhs, preferred_element_type=F32)

    kd = jnp.exp2(btot - b).astype(BF16) * kk
    upd = lax.dot_general(kd, v, (((0,), (0,)), ((), ())), preferred_element_type=F32)
    decay = jnp.broadcast_to(jnp.exp2(btot), (w, w)).T
    return o, st * decay + upd


def _hgrn_kernel(qf_ref, vf_ref, lff_ref, kf_ref, qb_ref, vb_ref, lfb_ref, kb_ref,
                 of_ref, ob_ref, st_ref, b_ref):
    @pl.when(pl.program_id(1) == 0)
    def _():
        st_ref[...] = jnp.zeros_like(st_ref)

    c = HG_CHUNK
    n_sub = qf_ref.shape[1] // c
    t_idx = lax.broadcasted_iota(jnp.int32, (c, c), 0)
    u_idx = lax.broadcasted_iota(jnp.int32, (c, c), 1)
    dirs = ((0, False, qf_ref, kf_ref, vf_ref, lff_ref, of_ref),
            (1, True, qb_ref, kb_ref, vb_ref, lfb_ref, ob_ref))

    widest = jnp.zeros((c // HG_BASE, 1, lff_ref.shape[2]), F32)
    for di, rev, _, _, _, lf_ref, _ in dirs:
        tri = ((u_idx >= t_idx) if rev else (u_idx <= t_idx)).astype(F32).astype(BF16)
        for j in range(n_sub):
            rows = slice(j * c, (j + 1) * c)
            bj = jnp.dot(tri, lf_ref[0, rows, :], preferred_element_type=F32) * LOG2E
            b_ref[di, rows, :] = bj
            b3 = bj.reshape(c // HG_BASE, HG_BASE, bj.shape[1])
            ref = _base_ref(b3, rev)
            widest = jnp.maximum(widest, jnp.maximum(jnp.abs(b3[:, 0:1] - ref),
                                                     jnp.abs(b3[:, HG_BASE - 1:HG_BASE] - ref)))
    fast = jnp.max(widest) <= HG_MAX_LOG2_RANGE

    def run(base):
        lev = {rev: _level_ids(c, rev, base) for rev in (False, True)}

        def body(j, carry):
            work = []
            for di, rev, q_ref, k_ref, v_ref, _, o_ref in dirs:
                r0 = pl.multiple_of(((n_sub - 1 - j) if rev else j) * c, c)
                for h in range(HG_HEADS):
                    cs = slice(h * HG_HEAD_DIM, (h + 1) * HG_HEAD_DIM)
                    ld = lambda ref: ref[0, pl.ds(r0, c), cs]
                    q, kk, bb = ld(q_ref), ld(k_ref), b_ref[di, pl.ds(r0, c), cs]
                    scores = _hgrn_scores(q, kk, bb, lev[rev], rev, base)
                    work.append((di, rev, h, r0, cs, q, kk, ld(v_ref), bb, scores, o_ref))
            for di, rev, h, r0, cs, q, kk, v, bb, scores, o_ref in work:
                o, st = _hgrn_output(scores, q, kk, v, bb, st_ref[di, h], rev)
                st_ref[di, h] = st
                o_ref[0, pl.ds(r0, c), cs] = o.astype(o_ref.dtype)
            return carry

        lax.fori_loop(0, n_sub, body, 0)

    pl.when(fast)(lambda: run(HG_BASE))
    pl.when(jnp.logical_not(fast))(lambda: run(1))


def _hgrn(hq, hv, lff, kf, lfb, kb):
    b, s, w = hq.shape
    ns = s // HG_STEP
    fwd = pl.BlockSpec((1, HG_STEP, w), lambda i, j: (i, j, 0))
    bwd = pl.BlockSpec((1, HG_STEP, w), lambda i, j: (i, ns - 1 - j, 0))
    sds = jax.ShapeDtypeStruct((b, s, w), BF16)
    return pl.pallas_call(
        _hgrn_kernel,
        grid=(b, ns),
        in_specs=[fwd, fwd, fwd, fwd, bwd, bwd, bwd, bwd],
        out_specs=[fwd, bwd],
        out_shape=[sds, sds],
        scratch_shapes=[pltpu.VMEM((2, HG_HEADS, HG_HEAD_DIM, HG_HEAD_DIM), F32),
                        pltpu.VMEM((2, HG_STEP, w), F32)],
        compiler_params=pltpu.CompilerParams(
            dimension_semantics=("parallel", "arbitrary"), vmem_limit_bytes=V7X_VMEM_LIMIT),
        name="hgrn",
    )(hq, hv, lff, kf, hq, hv, lfb, kb)


def _rms(x, g):
    ms = jnp.mean(x * x, axis=-1, keepdims=True)
    return x * lax.rsqrt(ms + NORM_EPS) * g


def _out_kernel(a_ref, of_ref, ob_ref, hg_ref, x_ref, p_ref,
                gh_ref, wo_ref, gp_ref, wg_ref, wp_ref, gf_ref, o_ref):
    sub = x_ref.shape[0] // OUT_SPLIT
    tiles = [slice(u * sub, (u + 1) * sub) for u in range(OUT_SPLIT)]
    gh = gh_ref[...]

    def mixed(rows):
        r = of_ref[rows, :].astype(F32) + ob_ref[rows, :].astype(F32)
        parts = [a_ref[rows, :]]
        for h in range(HG_HEADS):
            cs = slice(h * HG_HEAD_DIM, (h + 1) * HG_HEAD_DIM)
            y = _rms(r[:, cs], gh[:, cs]) * hg_ref[rows, cs].astype(F32)
            parts.append(y.astype(BF16))
        return jnp.concatenate(parts, axis=-1)

    cats = [mixed(rows) for rows in tiles]
    pps = [jnp.dot(p_ref[rows, :].astype(BF16), wp_ref[...], preferred_element_type=F32)
           for rows in tiles]
    h1s = [x_ref[rows, :] + jnp.dot(cat, wo_ref[...], preferred_element_type=F32)
           for rows, cat in zip(tiles, cats)]
    gates = [_sigmoid(jnp.dot(_rms(h1, gp_ref[...]).astype(BF16), wg_ref[...],
                              preferred_element_type=F32)) for h1 in h1s]
    for rows, h1, gate, pp in zip(tiles, h1s, gates, pps):
        o_ref[rows, :] = _rms(h1 + gate * pp, gf_ref[...])


def _out(a, o_f, o_b, hg, x2, p2, hgrn_norm, w_out, ln_ple, w_pg, w_pp, ln_final):
    n = x2.shape[0]
    row = lambda w: pl.BlockSpec((OUT_TM, w), lambda i: (i, 0))
    full = lambda r, c: pl.BlockSpec((r, c), lambda i: (0, 0))
    return pl.pallas_call(
        _out_kernel,
        grid=(n // OUT_TM,),
        in_specs=[row(NA_WIDTH), row(HG_WIDTH), row(HG_WIDTH), row(HG_WIDTH),
                  row(D_MODEL), row(PLE_DIM),
                  full(1, HG_WIDTH), full(D_MODEL, D_MODEL), full(1, D_MODEL),
                  full(D_MODEL, D_MODEL), full(PLE_DIM, D_MODEL), full(1, D_MODEL)],
        out_specs=row(D_MODEL),
        out_shape=jax.ShapeDtypeStruct((n, D_MODEL), F32),
        compiler_params=pltpu.CompilerParams(
            dimension_semantics=("parallel",), vmem_limit_bytes=V7X_VMEM_LIMIT),
        name="out",
    )(a, o_f, o_b, hg, x2, p2, hgrn_norm, w_out, ln_ple, w_pg, w_pp, ln_final)


def kernel(x, p, ln_mix, w_in, rpb, lb_logits, attn_norm, hgrn_norm, w_out, ln_ple, w_pg, w_pp, ln_final):
    depth = p.shape[0]
    assert depth == 1 and lb_logits.shape[0] == 2
    b, s, d = x.shape
    n = b * s
    rows = s // GRID_W
    assert d == D_MODEL and n % PROJ_TM == 0 and n % OUT_TM == 0 and s % HG_STEP == 0
    assert rows % NA_ROWS_PER_ITER == 0 and rows >= NA_ROWS

    x2 = x.reshape(n, d)
    (aq, ak, av, ag, hq, lff, kf, lfb, kb, hv, hg) = _proj(
        x2, ln_mix[0:1], w_in[0].astype(BF16), lb_logits)

    r3 = lambda t: t.reshape(b, s, t.shape[-1])
    a = _attn(rpb[0].reshape(-1), r3(aq), r3(ak), r3(av), r3(ag), attn_norm[0:1])
    o_f, o_b = _hgrn(r3(hq), r3(hv), r3(lff), r3(kf), r3(lfb), r3(kb))

    out = _out(a.reshape(n, NA_WIDTH), o_f.reshape(n, HG_WIDTH), o_b.reshape(n, HG_WIDTH), hg,
               x2, p[0].reshape(n, PLE_DIM), hgrn_norm[0:1], w_out[0].astype(BF16),
               ln_ple[0:1], w_pg[0].astype(BF16), w_pp[0].astype(BF16), ln_final.reshape(1, d))
    return out.reshape(b, s, d)
```

```python
import functools

import jax
import jax.numpy as jnp
import numpy as np
from jax import lax
from jax.experimental import pallas as pl
from jax.experimental.pallas import tpu as pltpu

F32 = jnp.float32
BF16 = jnp.bfloat16

D_MODEL = 1024
GRID_W = 64
NA_WIDTH = 512
HG_WIDTH = 512
NA_HEADS = 8
NA_HEAD_DIM = 64
NA_ROWS = 8
NA_COLS = 16
HG_HEAD_DIM = 128
HG_HEADS = 4
PLE_DIM = 256
NORM_EPS = 1e-6
N_GROUPS = 9
PROJ_TOTAL = N_GROUPS * 512

V7X_LANES = 128
V7X_VMEM_LIMIT = 56 * 1024 * 1024
MASK_VALUE = -1e30
LOG2E = 1.4426950408889634

PROJ_TM = 1024
PROJ_SPLIT = 4
OUT_TM = 1024
OUT_SPLIT = 4
HG_CHUNK = 128
HG_STEP = 512
HG_BASE = HG_CHUNK
HG_MAX_LOG2_RANGE = 110.0
NA_PAIRS = NA_HEADS // 2
NA_ROWS_PER_ITER = 16


def _sigmoid(x):
    return 1.0 / (1.0 + jnp.exp(-x))


def _proj_kernel(x_ref, g_ref, w_ref, lbl_ref,
                 aq_ref, ak_ref, av_ref, ag_ref, hq_ref,
                 lff_ref, kf_ref, lfb_ref, kb_ref, hv_ref, hg_ref):
    l0 = lbl_ref[0]
    l1 = lbl_ref[1]
    m = jnp.maximum(l0, l1)
    e0 = jnp.exp(l0 - m)
    e1 = jnp.exp(l1 - m)
    lb = e0 / (e0 + e1)

    sub = x_ref.shape[0] // PROJ_SPLIT
    tiles = [slice(u * sub, (u + 1) * sub) for u in range(PROJ_SPLIT)]
    hns = []
    for rows in tiles:
        x = x_ref[rows, :]
        ms = jnp.mean(x * x, axis=-1, keepdims=True)
        hns.append((x * lax.rsqrt(ms + NORM_EPS) * g_ref[...]).astype(BF16))

    def silu(t):
        return t * _sigmoid(t)

    def forget(lbj):
        def act(t):
            f = lbj + (1.0 - lbj) * _sigmoid(t)
            return jnp.log(f), 1.0 - f
        return act

    scale_q = lambda t: t * (NA_HEAD_DIM ** -0.5 * LOG2E)
    ident = lambda t: t
    groups = ((scale_q, (aq_ref,)), (ident, (ak_ref,)), (ident, (av_ref,)), (silu, (ag_ref,)),
              (silu, (hq_ref,)), (forget(lb[0:1, :]), (lff_ref, kf_ref)),
              (forget(lb[1:2, :]), (lfb_ref, kb_ref)), (ident, (hv_ref,)), (silu, (hg_ref,)))
    for j, (act, out_refs) in enumerate(groups):
        w = w_ref[:, j * 512:(j + 1) * 512]
        for rows, hn in zip(tiles, hns):
            res = act(jnp.dot(hn, w, preferred_element_type=F32))
            for ref, val in zip(out_refs, res if isinstance(res, tuple) else (res,)):
                ref[rows, :] = val.astype(BF16)


def _proj(x2, ln_mix, w_in, lb_logits):
    n = x2.shape[0]
    out_sds = jax.ShapeDtypeStruct((n, 512), BF16)
    row_spec = pl.BlockSpec((PROJ_TM, 512), lambda i: (i, 0))
    return pl.pallas_call(
        _proj_kernel,
        grid=(n // PROJ_TM,),
        in_specs=[
            pl.BlockSpec((PROJ_TM, D_MODEL), lambda i: (i, 0)),
            pl.BlockSpec((1, D_MODEL), lambda i: (0, 0)),
            pl.BlockSpec((D_MODEL, PROJ_TOTAL), lambda i: (0, 0), pipeline_mode=pl.Buffered(1)),
            pl.BlockSpec((2, 2, HG_WIDTH), lambda i: (0, 0, 0)),
        ],
        out_specs=[row_spec] * 11,
        out_shape=[out_sds] * 11,
        compiler_params=pltpu.CompilerParams(
            dimension_semantics=("parallel",), vmem_limit_bytes=V7X_VMEM_LIMIT),
        name="proj",
    )(x2, ln_mix, w_in, lb_logits)


def _attn_build_table(rpb_ref, tab_ref, pair):
    kh = tab_ref.shape[0]
    n_dr, n_dc = 2 * NA_ROWS - 1, 2 * NA_COLS - 1
    c_idx = lax.broadcasted_iota(jnp.int32, (GRID_W, V7X_LANES), 0)
    kc_idx = lax.broadcasted_iota(jnp.int32, (GRID_W, V7X_LANES), 1) % GRID_W
    start = jnp.clip(c_idx - NA_COLS // 2, 0, GRID_W - NA_COLS)
    valid = (kc_idx >= start) & (kc_idx < start + NA_COLS)
    diff = kc_idx - c_idx + NA_COLS - 1
    low_half = lax.broadcasted_iota(jnp.int32, (GRID_W, V7X_LANES), 1) < GRID_W
    for j in range(2):
        head = 2 * pair + j
        toeplitz = {}
        for dr in range(NA_ROWS - kh, NA_ROWS - 1 + kh):
            t = jnp.full((GRID_W, V7X_LANES), MASK_VALUE, F32)
            for dc in range(n_dc):
                t = jnp.where(diff == dc, rpb_ref[(head * n_dr + dr) * n_dc + dc] * LOG2E, t)
            toeplitz[dr] = jnp.where(valid, t, MASK_VALUE)
        for d in range(kh):
            for i in range(0, kh, 2):
                blk = jnp.where(low_half, toeplitz[i - d + NA_ROWS - 1], toeplitz[i + 1 - d + NA_ROWS - 1])
                tab_ref[d, j * GRID_W:(j + 1) * GRID_W, i * GRID_W:(i + 2) * GRID_W] = blk


def _attn_kernel(rpb_ref, q_ref, k_ref, v_ref, sg_ref, g_ref, o_ref, tab_ref):
    rows = q_ref.shape[1] // GRID_W
    kh = min(NA_ROWS, rows)
    nkeys = kh * GRID_W
    lane = lax.broadcasted_iota(jnp.int32, (GRID_W, V7X_LANES), 1)
    first = lane < NA_HEAD_DIM
    m0 = first.astype(F32).astype(BF16)
    m1 = (1.0 - first.astype(F32)).astype(BF16)
    gain = g_ref[...]

    @pl.when(pl.program_id(1) == 0)
    def _():
        _attn_build_table(rpb_ref, tab_ref, pl.program_id(0))

    def scores(r):
        rs = jnp.clip(r - kh // 2, 0, rows - kh)
        q0 = pl.multiple_of(r * GRID_W, GRID_W)
        k0 = pl.multiple_of(rs * GRID_W, GRID_W)
        q = q_ref[0, pl.ds(q0, GRID_W), :]
        kk = k_ref[0, pl.ds(k0, nkeys), :]
        qm = jnp.concatenate([m0 * q, m1 * q], axis=0)
        s = lax.dot_general(qm, kk, (((1,), (1,)), ((), ())), preferred_element_type=F32)
        return q0, k0, s + tab_ref[r - rs]

    def softmax(s):
        p = jnp.exp2(s - jnp.max(s, axis=-1, keepdims=True))
        return p.astype(BF16), jnp.sum(p, axis=-1, keepdims=True)

    def finish(q0, o2, den):
        o2 = o2 / den
        o = jnp.where(first, o2[:GRID_W], o2[GRID_W:])
        sq = o * o
        ms0 = jnp.sum(jnp.where(first, sq, 0.0), axis=-1, keepdims=True)
        ms1 = jnp.sum(jnp.where(first, 0.0, sq), axis=-1, keepdims=True)
        ms = jnp.where(first, ms0, ms1) * (1.0 / NA_HEAD_DIM)
        y = o * lax.rsqrt(ms + NORM_EPS) * gain
        y = y * sg_ref[0, pl.ds(q0, GRID_W), :].astype(F32)
        o_ref[0, pl.ds(q0, GRID_W), :] = y.astype(BF16)

    def body(i, carry):
        rs_ = [scores(i * NA_ROWS_PER_ITER + u) for u in range(NA_ROWS_PER_ITER)]
        ps = [softmax(s) for _, _, s in rs_]
        os_ = [jnp.dot(p, v_ref[0, pl.ds(k0, nkeys), :], preferred_element_type=F32)
               for (_, k0, _), (p, _) in zip(rs_, ps)]
        for (q0, _, _), (_, den), o2 in zip(rs_, ps, os_):
            finish(q0, o2, den)
        return carry

    lax.fori_loop(0, rows // NA_ROWS_PER_ITER, body, 0)


def _attn(rpb_flat, aq, ak, av, ag, attn_norm):
    b, s, _ = aq.shape
    kh = min(NA_ROWS, s // GRID_W)
    assert kh % 2 == 0
    blk = pl.BlockSpec((1, s, V7X_LANES), lambda hp, i: (i, 0, hp))
    return pl.pallas_call(
        _attn_kernel,
        grid=(NA_PAIRS, b),
        in_specs=[pl.BlockSpec(memory_space=pltpu.SMEM), blk, blk, blk, blk,
                  pl.BlockSpec((1, V7X_LANES), lambda hp, i: (0, hp))],
        out_specs=blk,
        out_shape=jax.ShapeDtypeStruct((b, s, NA_WIDTH), BF16),
        scratch_shapes=[pltpu.VMEM((kh, 2 * GRID_W, kh * GRID_W), F32)],
        compiler_params=pltpu.CompilerParams(
            dimension_semantics=("parallel", "arbitrary"), vmem_limit_bytes=V7X_VMEM_LIMIT),
        name="attn",
    )(rpb_flat, aq, ak, av, ag, attn_norm)


def _block_ref(b, n, rev):
    c, w = b.shape
    pos = n if rev else n - 1
    if 2 * n >= 8:
        b3 = b.reshape(c // (2 * n), 2 * n, w)
        return jnp.broadcast_to(b3[:, pos:pos + 1, :], b3.shape).reshape(c, w)
    row = lax.broadcasted_iota(jnp.int32, b.shape, 0)
    if n == 1:
        if rev:
            return jnp.where(row % 2 == 0, pltpu.roll(b, c - 1, 0), b)
        return jnp.where(row % 2 == 1, pltpu.roll(b, 1, 0), b)
    assert n == 2
    if rev:
        c1 = jnp.where(row % 2 == 1, pltpu.roll(b, 1, 0), b)
        return jnp.where(row % 4 < 2, pltpu.roll(c1, c - 2, 0), c1)
    c1 = jnp.where(row % 2 == 0, pltpu.roll(b, c - 1, 0), b)
    return jnp.where(row % 4 >= 2, pltpu.roll(c1, 2, 0), c1)


def _level_decay(b, n, rev):
    c, w = b.shape
    if n < 8:
        return jnp.exp2(-jnp.abs(b - _block_ref(b, n, rev)))
    b3 = b.reshape(c // (2 * n), 2 * n, w)
    lo, hi = b3[:, :n], b3[:, n:]
    if rev:
        ref = b3[:, n:n + 1]
        parts = [jnp.exp2(lo - ref), jnp.exp2(ref - hi)]
    else:
        ref = b3[:, n - 1:n]
        parts = [jnp.exp2(ref - lo), jnp.exp2(hi - ref)]
    return jnp.concatenate(parts, axis=1).reshape(c, w)


def _base_ref(b3, rev):
    base = b3.shape[1]
    pos = base // 2 if rev else base // 2 - 1
    return b3[:, pos:pos + 1]


def _level_ids(c, rev, base):
    t_idx = lax.broadcasted_iota(jnp.int32, (c, c), 0)
    s_idx = lax.broadcasted_iota(jnp.int32, (c, c), 1)
    x = t_idx ^ s_idx
    lev = jnp.ones((c, c), jnp.int32)
    n = 2 * base
    while n < c:
        lev = lev + (x >= n).astype(jnp.int32)
        n *= 2
    before = (t_idx < s_idx) if rev else (t_idx > s_idx)
    lev = jnp.where(before & (x >= base), lev, -1)
    return jnp.where((x < base) & (before | (t_idx == s_idx)), 0, lev)


def _hgrn_scores(q, kk, b, lev, rev, base):
    c, w = q.shape
    nt = (((1,), (1,)), ((), ()))

    if base == 1:
        scores = lax.dot_general(q, kk, nt, preferred_element_type=F32)
    else:
        b3 = b.reshape(c // base, base, w)
        d = (b3 - _base_ref(b3, rev)).reshape(c, w)
        scores = lax.dot_general(jnp.exp2(d).astype(BF16) * q, jnp.exp2(-d).astype(BF16) * kk, nt,
                                 preferred_element_type=F32)
    scores = jnp.where(lev == 0, scores, 0.0)
    n, li = base, 1
    while n < c:
        e = _level_decay(b, n, rev).astype(BF16)
        sl = lax.dot_general(e * q, e * kk, nt, preferred_element_type=F32)
        scores = jnp.where(lev == li, sl, scores)
        n, li = 2 * n, li + 1
    return scores.astype(BF16)


def _hgrn_output(scores, q, kk, v, b, st, rev):
    c, w = q.shape
    btot = b[0:1, :] if rev else b[c - 1:c, :]
    lhs = jnp.concatenate([scores, jnp.exp2(b).astype(BF16) * q], axis=1)
    rhs = jnp.concatenate([v, st.astype(BF16)], axis=0)
    o = jnp.dot(lhs, rhs, preferred_element_type=F32)

    kd = jnp.exp2(btot - b).astype(BF16) * kk
    upd = lax.dot_general(kd, v, (((0,), (0,)), ((), ())), preferred_element_type=F32)
    decay = jnp.broadcast_to(jnp.exp2(btot), (w, w)).T
    return o, st * decay + upd


def _hgrn_kernel(qf_ref, vf_ref, lff_ref, kf_ref, qb_ref, vb_ref, lfb_ref, kb_ref,
                 of_ref, ob_ref, st_ref, b_ref):
    @pl.when(pl.program_id(1) == 0)
    def _():
        st_ref[...] = jnp.zeros_like(st_ref)

    c = HG_CHUNK
    n_sub = qf_ref.shape[1] // c
    t_idx = lax.broadcasted_iota(jnp.int32, (c, c), 0)
    u_idx = lax.broadcasted_iota(jnp.int32, (c, c), 1)
    dirs = ((0, False, qf_ref, kf_ref, vf_ref, lff_ref, of_ref),
            (1, True, qb_ref, kb_ref, vb_ref, lfb_ref, ob_ref))

    widest = jnp.zeros((c // HG_BASE, 1, lff_ref.shape[2]), F32)
    for di, rev, _, _, _, lf_ref, _ in dirs:
        tri = ((u_idx >= t_idx) if rev else (u_idx <= t_idx)).astype(F32).astype(BF16)
        for j in range(n_sub):
            rows = slice(j * c, (j + 1) * c)
            bj = jnp.dot(tri, lf_ref[0, rows, :], preferred_element_type=F32) * LOG2E
            b_ref[di, rows, :] = bj
            b3 = bj.reshape(c // HG_BASE, HG_BASE, bj.shape[1])
            ref = _base_ref(b3, rev)
            widest = jnp.maximum(widest, jnp.maximum(jnp.abs(b3[:, 0:1] - ref),
                                                     jnp.abs(b3[:, HG_BASE - 1:HG_BASE] - ref)))
    fast = jnp.max(widest) <= HG_MAX_LOG2_RANGE

    def run(base):
        lev = {rev: _level_ids(c, rev, base) for rev in (False, True)}

        def body(j, carry):
            work = []
            for di, rev, q_ref, k_ref, v_ref, _, o_ref in dirs:
                r0 = ((n_sub - 1 - j) if rev else j) * c
                if not isinstance(j, int):
                    r0 = pl.multiple_of(r0, c)
                for h in range(HG_HEADS):
                    cs = slice(h * HG_HEAD_DIM, (h + 1) * HG_HEAD_DIM)
                    ld = lambda ref: ref[0, pl.ds(r0, c), cs]
                    q, kk, bb = ld(q_ref), ld(k_ref), b_ref[di, pl.ds(r0, c), cs]
                    scores = _hgrn_scores(q, kk, bb, lev[rev], rev, base)
                    work.append((di, rev, h, r0, cs, q, kk, ld(v_ref), bb, scores, o_ref))
            for di, rev, h, r0, cs, q, kk, v, bb, scores, o_ref in work:
                o, st = _hgrn_output(scores, q, kk, v, bb, st_ref[di, h], rev)
                st_ref[di, h] = st
                o_ref[0, pl.ds(r0, c), cs] = o.astype(o_ref.dtype)
            return carry

        if base == 1:
            lax.fori_loop(0, n_sub, body, 0)
        else:
            for j in range(n_sub):
                body(j, 0)

    pl.when(fast)(lambda: run(HG_BASE))
    pl.when(jnp.logical_not(fast))(lambda: run(1))


def _hgrn(hq, hv, lff, kf, lfb, kb):
    b, s, w = hq.shape
    ns = s // HG_STEP
    fwd = pl.BlockSpec((1, HG_STEP, w), lambda i, j: (i, j, 0))
    bwd = pl.BlockSpec((1, HG_STEP, w), lambda i, j: (i, ns - 1 - j, 0))
    sds = jax.ShapeDtypeStruct((b, s, w), BF16)
    return pl.pallas_call(
        _hgrn_kernel,
        grid=(b, ns),
        in_specs=[fwd, fwd, fwd, fwd, bwd, bwd, bwd, bwd],
        out_specs=[fwd, bwd],
        out_shape=[sds, sds],
        scratch_shapes=[pltpu.VMEM((2, HG_HEADS, HG_HEAD_DIM, HG_HEAD_DIM), F32),
                        pltpu.VMEM((2, HG_STEP, w), F32)],
        compiler_params=pltpu.CompilerParams(
            dimension_semantics=("parallel", "arbitrary"), vmem_limit_bytes=V7X_VMEM_LIMIT),
        name="hgrn",
    )(hq, hv, lff, kf, hq, hv, lfb, kb)


def _rms(x, g):
    ms = jnp.mean(x * x, axis=-1, keepdims=True)
    return x * lax.rsqrt(ms + NORM_EPS) * g


def _out_kernel(a_ref, of_ref, ob_ref, hg_ref, x_ref, p_ref,
                gh_ref, wo_ref, gp_ref, wg_ref, wp_ref, gf_ref, o_ref):
    sub = x_ref.shape[0] // OUT_SPLIT
    tiles = [slice(u * sub, (u + 1) * sub) for u in range(OUT_SPLIT)]
    gh = gh_ref[...]

    def mixed(rows):
        r = of_ref[rows, :].astype(F32) + ob_ref[rows, :].astype(F32)
        parts = [a_ref[rows, :]]
        for h in range(HG_HEADS):
            cs = slice(h * HG_HEAD_DIM, (h + 1) * HG_HEAD_DIM)
            y = _rms(r[:, cs], gh[:, cs]) * hg_ref[rows, cs].astype(F32)
            parts.append(y.astype(BF16))
        return jnp.concatenate(parts, axis=-1)

    cats = [mixed(rows) for rows in tiles]
    pps = [jnp.dot(p_ref[rows, :].astype(BF16), wp_ref[...], preferred_element_type=F32)
           for rows in tiles]
    h1s = [x_ref[rows, :] + jnp.dot(cat, wo_ref[...], preferred_element_type=F32)
           for rows, cat in zip(tiles, cats)]
    gates = [_sigmoid(jnp.dot(_rms(h1, gp_ref[...]).astype(BF16), wg_ref[...],
                              preferred_element_type=F32)) for h1 in h1s]
    for rows, h1, gate, pp in zip(tiles, h1s, gates, pps):
        o_ref[rows, :] = _rms(h1 + gate * pp, gf_ref[...])


def _out(a, o_f, o_b, hg, x2, p2, hgrn_norm, w_out, ln_ple, w_pg, w_pp, ln_final):
    n = x2.shape[0]
    row = lambda w: pl.BlockSpec((OUT_TM, w), lambda i: (i, 0))
    full = lambda r, c: pl.BlockSpec((r, c), lambda i: (0, 0))
    return pl.pallas_call(
        _out_kernel,
        grid=(n // OUT_TM,),
        in_specs=[row(NA_WIDTH), row(HG_WIDTH), row(HG_WIDTH), row(HG_WIDTH),
                  row(D_MODEL), row(PLE_DIM),
                  full(1, HG_WIDTH), full(D_MODEL, D_MODEL), full(1, D_MODEL),
                  full(D_MODEL, D_MODEL), full(PLE_DIM, D_MODEL), full(1, D_MODEL)],
        out_specs=row(D_MODEL),
        out_shape=jax.ShapeDtypeStruct((n, D_MODEL), F32),
        compiler_params=pltpu.CompilerParams(
            dimension_semantics=("parallel",), vmem_limit_bytes=V7X_VMEM_LIMIT),
        name="out",
    )(a, o_f, o_b, hg, x2, p2, hgrn_norm, w_out, ln_ple, w_pg, w_pp, ln_final)


def kernel(x, p, ln_mix, w_in, rpb, lb_logits, attn_norm, hgrn_norm, w_out, ln_ple, w_pg, w_pp, ln_final):
    depth = p.shape[0]
    assert depth == 1 and lb_logits.shape[0] == 2
    b, s, d = x.shape
    n = b * s
    rows = s // GRID_W
    assert d == D_MODEL and n % PROJ_TM == 0 and n % OUT_TM == 0 and s % HG_STEP == 0
    assert rows % NA_ROWS_PER_ITER == 0 and rows >= NA_ROWS

    x2 = x.reshape(n, d)
    (aq, ak, av, ag, hq, lff, kf, lfb, kb, hv, hg) = _proj(
        x2, ln_mix[0:1], w_in[0].astype(BF16), lb_logits)

    r3 = lambda t: t.reshape(b, s, t.shape[-1])
    a = _attn(rpb[0].reshape(-1), r3(aq), r3(ak), r3(av), r3(ag), attn_norm[0:1])
    o_f, o_b = _hgrn(r3(hq), r3(hv), r3(lff), r3(kf), r3(lfb), r3(kb))

    out = _out(a.reshape(n, NA_WIDTH), o_f.reshape(n, HG_WIDTH), o_b.reshape(n, HG_WIDTH), hg,
               x2, p[0].reshape(n, PLE_DIM), hgrn_norm[0:1], w_out[0].astype(BF16),
               ln_ple[0:1], w_pg[0].astype(BF16), w_pp[0].astype(BF16), ln_final.reshape(1, d))
    return out.reshape(b, s, d)
```

```python
import jax
import jax.numpy as jnp
from jax import lax
from jax.experimental import pallas as pl
from jax.experimental.pallas import tpu as pltpu

F32 = jnp.float32
BF16 = jnp.bfloat16

D_MODEL = 1024
GRID_W = 64
NA_WIDTH = 512
HG_WIDTH = 512
NA_HEADS = 8
NA_HEAD_DIM = 64
NA_ROWS = 8
NA_COLS = 16
HG_HEAD_DIM = 128
HG_HEADS = 4
PLE_DIM = 256
NORM_EPS = 1e-6
N_GROUPS = 9
PROJ_TOTAL = N_GROUPS * 512

V7X_LANES = 128
V7X_VMEM_LIMIT = 56 * 1024 * 1024
MASK_VALUE = -1e30
LOG2E = 1.4426950408889634

PROJ_TM = 1024
PROJ_SPLIT = 4
OUT_TM = 1024
OUT_SPLIT = 4
HG_CHUNK = 128
HG_STEP = 512
HG_BASE = HG_CHUNK
HG_MAX_LOG2_RANGE = 110.0
NA_PAIRS = NA_HEADS // 2
NA_ROWS_PER_ITER = 32


def _sigmoid(x):
    return 1.0 / (1.0 + jnp.exp(-x))


def _proj_kernel(x_ref, g_ref, w_ref, lbl_ref,
                 aq_ref, ak_ref, av_ref, ag_ref, hq_ref,
                 lff_ref, kf_ref, lfb_ref, kb_ref, hv_ref, hg_ref):
    l0 = lbl_ref[0]
    l1 = lbl_ref[1]
    m = jnp.maximum(l0, l1)
    e0 = jnp.exp(l0 - m)
    e1 = jnp.exp(l1 - m)
    lb = e0 / (e0 + e1)

    sub = x_ref.shape[0] // PROJ_SPLIT
    tiles = [slice(u * sub, (u + 1) * sub) for u in range(PROJ_SPLIT)]
    hns = []
    for rows in tiles:
        x = x_ref[rows, :]
        ms = jnp.mean(x * x, axis=-1, keepdims=True)
        hns.append((x * lax.rsqrt(ms + NORM_EPS) * g_ref[...]).astype(BF16))

    def silu(t):
        return t * _sigmoid(t)

    def forget(lbj):
        def act(t):
            f = lbj + (1.0 - lbj) * _sigmoid(t)
            return jnp.log(f), 1.0 - f
        return act

    scale_q = lambda t: t * (NA_HEAD_DIM ** -0.5 * LOG2E)
    ident = lambda t: t
    groups = ((scale_q, (aq_ref,)), (ident, (ak_ref,)), (ident, (av_ref,)), (silu, (ag_ref,)),
              (silu, (hq_ref,)), (forget(lb[0:1, :]), (lff_ref, kf_ref)),
              (forget(lb[1:2, :]), (lfb_ref, kb_ref)), (ident, (hv_ref,)), (silu, (hg_ref,)))
    for j, (act, out_refs) in enumerate(groups):
        w = w_ref[:, j * 512:(j + 1) * 512]
        for rows, hn in zip(tiles, hns):
            res = act(jnp.dot(hn, w, preferred_element_type=F32))
            for ref, val in zip(out_refs, res if isinstance(res, tuple) else (res,)):
                ref[rows, :] = val.astype(BF16)


def _proj(x2, ln_mix, w_in, lb_logits):
    n = x2.shape[0]
    out_sds = jax.ShapeDtypeStruct((n, 512), BF16)
    row_spec = pl.BlockSpec((PROJ_TM, 512), lambda i: (i, 0))
    return pl.pallas_call(
        _proj_kernel,
        grid=(n // PROJ_TM,),
        in_specs=[
            pl.BlockSpec((PROJ_TM, D_MODEL), lambda i: (i, 0)),
            pl.BlockSpec((1, D_MODEL), lambda i: (0, 0)),
            pl.BlockSpec((D_MODEL, PROJ_TOTAL), lambda i: (0, 0), pipeline_mode=pl.Buffered(1)),
            pl.BlockSpec((2, 2, HG_WIDTH), lambda i: (0, 0, 0)),
        ],
        out_specs=[row_spec] * 11,
        out_shape=[out_sds] * 11,
        compiler_params=pltpu.CompilerParams(
            dimension_semantics=("parallel",), vmem_limit_bytes=V7X_VMEM_LIMIT),
        name="proj",
    )(x2, ln_mix, w_in, lb_logits)


def _attn_build_table(rpb_ref, tab_ref, pair):
    kh = tab_ref.shape[0]
    n_dr, n_dc = 2 * NA_ROWS - 1, 2 * NA_COLS - 1
    c_idx = lax.broadcasted_iota(jnp.int32, (GRID_W, V7X_LANES), 0)
    kc_idx = lax.broadcasted_iota(jnp.int32, (GRID_W, V7X_LANES), 1) % GRID_W
    start = jnp.clip(c_idx - NA_COLS // 2, 0, GRID_W - NA_COLS)
    valid = (kc_idx >= start) & (kc_idx < start + NA_COLS)
    diff = kc_idx - c_idx + NA_COLS - 1
    low_half = lax.broadcasted_iota(jnp.int32, (GRID_W, V7X_LANES), 1) < GRID_W
    for j in range(2):
        head = 2 * pair + j
        toeplitz = {}
        for dr in range(NA_ROWS - kh, NA_ROWS - 1 + kh):
            t = jnp.full((GRID_W, V7X_LANES), MASK_VALUE, F32)
            for dc in range(n_dc):
                t = jnp.where(diff == dc, rpb_ref[(head * n_dr + dr) * n_dc + dc] * LOG2E, t)
            toeplitz[dr] = jnp.where(valid, t, MASK_VALUE)
        for d in range(kh):
            for i in range(0, kh, 2):
                blk = jnp.where(low_half, toeplitz[i - d + NA_ROWS - 1], toeplitz[i + 1 - d + NA_ROWS - 1])
                tab_ref[d, j * GRID_W:(j + 1) * GRID_W, i * GRID_W:(i + 2) * GRID_W] = blk


def _attn_kernel(rpb_ref, q_ref, k_ref, v_ref, sg_ref, g_ref, o_ref, tab_ref):
    rows = q_ref.shape[1] // GRID_W
    kh = min(NA_ROWS, rows)
    nkeys = kh * GRID_W
    lane = lax.broadcasted_iota(jnp.int32, (GRID_W, V7X_LANES), 1)
    first = lane < NA_HEAD_DIM
    m0 = first.astype(F32).astype(BF16)
    m1 = (1.0 - first.astype(F32)).astype(BF16)
    gain = g_ref[...]

    @pl.when(pl.program_id(1) == 0)
    def _():
        _attn_build_table(rpb_ref, tab_ref, pl.program_id(0))

    def scores(r):
        rs = jnp.clip(r - kh // 2, 0, rows - kh)
        q0 = pl.multiple_of(r * GRID_W, GRID_W)
        k0 = pl.multiple_of(rs * GRID_W, GRID_W)
        q = q_ref[0, pl.ds(q0, GRID_W), :]
        kk = k_ref[0, pl.ds(k0, nkeys), :]
        qm = jnp.concatenate([m0 * q, m1 * q], axis=0)
        s = lax.dot_general(qm, kk, (((1,), (1,)), ((), ())), preferred_element_type=F32)
        return q0, k0, s + tab_ref[r - rs]

    def softmax(s):
        p = jnp.exp2(s - jnp.max(s, axis=-1, keepdims=True))
        return p.astype(BF16), jnp.sum(p, axis=-1, keepdims=True)

    def finish(q0, o2, den):
        o2 = o2 / den
        o = jnp.where(first, o2[:GRID_W], o2[GRID_W:])
        sq = o * o
        ms0 = jnp.sum(jnp.where(first, sq, 0.0), axis=-1, keepdims=True)
        ms1 = jnp.sum(jnp.where(first, 0.0, sq), axis=-1, keepdims=True)
        ms = jnp.where(first, ms0, ms1) * (1.0 / NA_HEAD_DIM)
        y = o * lax.rsqrt(ms + NORM_EPS) * gain
        y = y * sg_ref[0, pl.ds(q0, GRID_W), :].astype(F32)
        o_ref[0, pl.ds(q0, GRID_W), :] = y.astype(BF16)

    per_iter = min(NA_ROWS_PER_ITER, rows)

    def body(i, carry):
        rs_ = [scores(i * per_iter + u) for u in range(per_iter)]
        ps = [softmax(s) for _, _, s in rs_]
        os_ = [jnp.dot(p, v_ref[0, pl.ds(k0, nkeys), :], preferred_element_type=F32)
               for (_, k0, _), (p, _) in zip(rs_, ps)]
        for (q0, _, _), (_, den), o2 in zip(rs_, ps, os_):
            finish(q0, o2, den)
        return carry

    lax.fori_loop(0, rows // per_iter, body, 0)


def _attn(rpb_flat, aq, ak, av, ag, attn_norm):
    b, s, _ = aq.shape
    kh = min(NA_ROWS, s // GRID_W)
    assert kh % 2 == 0
    blk = pl.BlockSpec((1, s, V7X_LANES), lambda hp, i: (i, 0, hp))
    return pl.pallas_call(
        _attn_kernel,
        grid=(NA_PAIRS, b),
        in_specs=[pl.BlockSpec(memory_space=pltpu.SMEM), blk, blk, blk, blk,
                  pl.BlockSpec((1, V7X_LANES), lambda hp, i: (0, hp))],
        out_specs=blk,
        out_shape=jax.ShapeDtypeStruct((b, s, NA_WIDTH), BF16),
        scratch_shapes=[pltpu.VMEM((kh, 2 * GRID_W, kh * GRID_W), F32)],
        compiler_params=pltpu.CompilerParams(
            dimension_semantics=("parallel", "arbitrary"), vmem_limit_bytes=V7X_VMEM_LIMIT),
        name="attn",
    )(rpb_flat, aq, ak, av, ag, attn_norm)


def _block_ref(b, n, rev):
    c, w = b.shape
    pos = n if rev else n - 1
    if 2 * n >= 8:
        b3 = b.reshape(c // (2 * n), 2 * n, w)
        return jnp.broadcast_to(b3[:, pos:pos + 1, :], b3.shape).reshape(c, w)
    row = lax.broadcasted_iota(jnp.int32, b.shape, 0)
    if n == 1:
        if rev:
            return jnp.where(row % 2 == 0, pltpu.roll(b, c - 1, 0), b)
        return jnp.where(row % 2 == 1, pltpu.roll(b, 1, 0), b)
    assert n == 2
    if rev:
        c1 = jnp.where(row % 2 == 1, pltpu.roll(b, 1, 0), b)
        return jnp.where(row % 4 < 2, pltpu.roll(c1, c - 2, 0), c1)
    c1 = jnp.where(row % 2 == 0, pltpu.roll(b, c - 1, 0), b)
    return jnp.where(row % 4 >= 2, pltpu.roll(c1, 2, 0), c1)


def _level_decay(b, n, rev):
    c, w = b.shape
    if n < 8:
        return jnp.exp2(-jnp.abs(b - _block_ref(b, n, rev)))
    b3 = b.reshape(c // (2 * n), 2 * n, w)
    lo, hi = b3[:, :n], b3[:, n:]
    if rev:
        ref = b3[:, n:n + 1]
        parts = [jnp.exp2(lo - ref), jnp.exp2(ref - hi)]
    else:
        ref = b3[:, n - 1:n]
        parts = [jnp.exp2(ref - lo), jnp.exp2(hi - ref)]
    return jnp.concatenate(parts, axis=1).reshape(c, w)


def _base_ref(b3, rev):
    base = b3.shape[1]
    pos = base // 2 if rev else base // 2 - 1
    return b3[:, pos:pos + 1]


def _level_ids(c, rev, base):
    t_idx = lax.broadcasted_iota(jnp.int32, (c, c), 0)
    s_idx = lax.broadcasted_iota(jnp.int32, (c, c), 1)
    x = t_idx ^ s_idx
    lev = jnp.ones((c, c), jnp.int32)
    n = 2 * base
    while n < c:
        lev = lev + (x >= n).astype(jnp.int32)
        n *= 2
    before = (t_idx < s_idx) if rev else (t_idx > s_idx)
    lev = jnp.where(before & (x >= base), lev, -1)
    return jnp.where((x < base) & (before | (t_idx == s_idx)), 0, lev)


def _hgrn_scores(q, kk, b, lev, rev, base):
    c, w = q.shape
    nt = (((1,), (1,)), ((), ()))

    if base == 1:
        scores = lax.dot_general(q, kk, nt, preferred_element_type=F32)
    else:
        b3 = b.reshape(c // base, base, w)
        d = (b3 - _base_ref(b3, rev)).reshape(c, w)
        scores = lax.dot_general(jnp.exp2(d).astype(BF16) * q, jnp.exp2(-d).astype(BF16) * kk, nt,
                                 preferred_element_type=F32)
    scores = jnp.where(lev == 0, scores, 0.0)
    n, li = base, 1
    while n < c:
        e = _level_decay(b, n, rev).astype(BF16)
        sl = lax.dot_general(e * q, e * kk, nt, preferred_element_type=F32)
        scores = jnp.where(lev == li, sl, scores)
        n, li = 2 * n, li + 1
    return scores.astype(BF16)


def _hgrn_output(scores, q, kk, v, b, st, rev):
    c, w = q.shape
    btot = b[0:1, :] if rev else b[c - 1:c, :]
    lhs = jnp.concatenate([scores, jnp.exp2(b).astype(BF16) * q], axis=1)
    rhs = jnp.concatenate([v, st.astype(BF16)], axis=0)
    o = jnp.dot(lhs, rhs, preferred_element_type=F32)

    kd = jnp.exp2(btot - b).astype(BF16) * kk
    upd = lax.dot_general(kd, v, (((0,), (0,)), ((), ())), preferred_element_type=F32)
    decay = jnp.broadcast_to(jnp.exp2(btot), (w, w)).T
    return o, st * decay + upd


def _hgrn_kernel(qf_ref, vf_ref, lff_ref, kf_ref, qb_ref, vb_ref, lfb_ref, kb_ref,
                 of_ref, ob_ref, st_ref, b_ref):
    @pl.when(pl.program_id(1) == 0)
    def _():
        st_ref[...] = jnp.zeros_like(st_ref)

    c = HG_CHUNK
    n_sub = qf_ref.shape[1] // c
    t_idx = lax.broadcasted_iota(jnp.int32, (c, c), 0)
    u_idx = lax.broadcasted_iota(jnp.int32, (c, c), 1)
    dirs = ((0, False, qf_ref, kf_ref, vf_ref, lff_ref, of_ref),
            (1, True, qb_ref, kb_ref, vb_ref, lfb_ref, ob_ref))

    widest = jnp.zeros((c // HG_BASE, 1, lff_ref.shape[2]), F32)
    for di, rev, _, _, _, lf_ref, _ in dirs:
        tri = ((u_idx >= t_idx) if rev else (u_idx <= t_idx)).astype(F32).astype(BF16)
        for j in range(n_sub):
            rows = slice(j * c, (j + 1) * c)
            bj = jnp.dot(tri, lf_ref[0, rows, :], preferred_element_type=F32) * LOG2E
            b_ref[di, rows, :] = bj
            b3 = bj.reshape(c // HG_BASE, HG_BASE, bj.shape[1])
            ref = _base_ref(b3, rev)
            widest = jnp.maximum(widest, jnp.maximum(jnp.abs(b3[:, 0:1] - ref),
                                                     jnp.abs(b3[:, HG_BASE - 1:HG_BASE] - ref)))
    fast = jnp.max(widest) <= HG_MAX_LOG2_RANGE

    def run(base):
        lev = {rev: _level_ids(c, rev, base) for rev in (False, True)}

        def body(j, carry):
            work = []
            for di, rev, q_ref, k_ref, v_ref, _, o_ref in dirs:
                r0 = ((n_sub - 1 - j) if rev else j) * c
                if not isinstance(j, int):
                    r0 = pl.multiple_of(r0, c)
                for h in range(HG_HEADS):
                    cs = slice(h * HG_HEAD_DIM, (h + 1) * HG_HEAD_DIM)
                    ld = lambda ref: ref[0, pl.ds(r0, c), cs]
                    q, kk, bb = ld(q_ref), ld(k_ref), b_ref[di, pl.ds(r0, c), cs]
                    scores = _hgrn_scores(q, kk, bb, lev[rev], rev, base)
                    work.append((di, rev, h, r0, cs, q, kk, ld(v_ref), bb, scores, o_ref))
            for di, rev, h, r0, cs, q, kk, v, bb, scores, o_ref in work:
                o, st = _hgrn_output(scores, q, kk, v, bb, st_ref[di, h], rev)
                st_ref[di, h] = st
                o_ref[0, pl.ds(r0, c), cs] = o.astype(o_ref.dtype)
            return carry

        if base == 1:
            lax.fori_loop(0, n_sub, body, 0)
        else:
            for j in range(n_sub):
                body(j, 0)

    pl.when(fast)(lambda: run(HG_BASE))
    pl.when(jnp.logical_not(fast))(lambda: run(1))


def _hgrn(hq, hv, lff, kf, lfb, kb):
    b, s, w = hq.shape
    ns = s // HG_STEP
    fwd = pl.BlockSpec((1, HG_STEP, w), lambda i, j: (i, j, 0))
    bwd = pl.BlockSpec((1, HG_STEP, w), lambda i, j: (i, ns - 1 - j, 0))
    sds = jax.ShapeDtypeStruct((b, s, w), BF16)
    return pl.pallas_call(
        _hgrn_kernel,
        grid=(b, ns),
        in_specs=[fwd, fwd, fwd, fwd, bwd, bwd, bwd, bwd],
        out_specs=[fwd, bwd],
        out_shape=[sds, sds],
        scratch_shapes=[pltpu.VMEM((2, HG_HEADS, HG_HEAD_DIM, HG_HEAD_DIM), F32),
                        pltpu.VMEM((2, HG_STEP, w), F32)],
        compiler_params=pltpu.CompilerParams(
            dimension_semantics=("parallel", "arbitrary"), vmem_limit_bytes=V7X_VMEM_LIMIT),
        name="hgrn",
    )(hq, hv, lff, kf, hq, hv, lfb, kb)


def _rms(x, g):
    ms = jnp.mean(x * x, axis=-1, keepdims=True)
    return x * lax.rsqrt(ms + NORM_EPS) * g


def _out_kernel(a_ref, of_ref, ob_ref, hg_ref, x_ref, p_ref,
                gh_ref, wo_ref, gp_ref, wg_ref, wp_ref, gf_ref, o_ref):
    sub = x_ref.shape[0] // OUT_SPLIT
    tiles = [slice(u * sub, (u + 1) * sub) for u in range(OUT_SPLIT)]
    gh = gh_ref[...]

    def mixed(rows):
        r = of_ref[rows, :].astype(F32) + ob_ref[rows, :].astype(F32)
        parts = [a_ref[rows, :]]
        for h in range(HG_HEADS):
            cs = slice(h * HG_HEAD_DIM, (h + 1) * HG_HEAD_DIM)
            y = _rms(r[:, cs], gh[:, cs]) * hg_ref[rows, cs].astype(F32)
            parts.append(y.astype(BF16))
        return jnp.concatenate(parts, axis=-1)

    cats = [mixed(rows) for rows in tiles]
    pps = [jnp.dot(p_ref[rows, :].astype(BF16), wp_ref[...], preferred_element_type=F32)
           for rows in tiles]
    h1s = [x_ref[rows, :] + jnp.dot(cat, wo_ref[...], preferred_element_type=F32)
           for rows, cat in zip(tiles, cats)]
    gates = [_sigmoid(jnp.dot(_rms(h1, gp_ref[...]).astype(BF16), wg_ref[...],
                              preferred_element_type=F32)) for h1 in h1s]
    for rows, h1, gate, pp in zip(tiles, h1s, gates, pps):
        o_ref[rows, :] = _rms(h1 + gate * pp, gf_ref[...])


def _out(a, o_f, o_b, hg, x2, p2, hgrn_norm, w_out, ln_ple, w_pg, w_pp, ln_final):
    n = x2.shape[0]
    row = lambda w: pl.BlockSpec((OUT_TM, w), lambda i: (i, 0))
    full = lambda r, c: pl.BlockSpec((r, c), lambda i: (0, 0))
    return pl.pallas_call(
        _out_kernel,
        grid=(n // OUT_TM,),
        in_specs=[row(NA_WIDTH), row(HG_WIDTH), row(HG_WIDTH), row(HG_WIDTH),
                  row(D_MODEL), row(PLE_DIM),
                  full(1, HG_WIDTH), full(D_MODEL, D_MODEL), full(1, D_MODEL),
                  full(D_MODEL, D_MODEL), full(PLE_DIM, D_MODEL), full(1, D_MODEL)],
        out_specs=row(D_MODEL),
        out_shape=jax.ShapeDtypeStruct((n, D_MODEL), F32),
        compiler_params=pltpu.CompilerParams(
            dimension_semantics=("parallel",), vmem_limit_bytes=V7X_VMEM_LIMIT),
        name="out",
    )(a, o_f, o_b, hg, x2, p2, hgrn_norm, w_out, ln_ple, w_pg, w_pp, ln_final)


def kernel(x, p, ln_mix, w_in, rpb, lb_logits, attn_norm, hgrn_norm, w_out, ln_ple, w_pg, w_pp, ln_final):
    depth = p.shape[0]
    assert depth == 1 and lb_logits.shape[0] == 2
    b, s, d = x.shape
    n = b * s
    rows = s // GRID_W
    assert d == D_MODEL and n % PROJ_TM == 0 and n % OUT_TM == 0 and s % HG_STEP == 0
    assert rows % min(NA_ROWS_PER_ITER, rows) == 0 and rows >= NA_ROWS

    x2 = x.reshape(n, d)
    (aq, ak, av, ag, hq, lff, kf, lfb, kb, hv, hg) = _proj(
        x2, ln_mix[0:1], w_in[0].astype(BF16), lb_logits)

    r3 = lambda t: t.reshape(b, s, t.shape[-1])
    a = _attn(rpb[0].reshape(-1), r3(aq), r3(ak), r3(av), r3(ag), attn_norm[0:1])
    o_f, o_b = _hgrn(r3(hq), r3(hv), r3(lff), r3(kf), r3(lfb), r3(kb))

    out = _out(a.reshape(n, NA_WIDTH), o_f.reshape(n, HG_WIDTH), o_b.reshape(n, HG_WIDTH), hg,
               x2, p[0].reshape(n, PLE_DIM), hgrn_norm[0:1], w_out[0].astype(BF16),
               ln_ple[0:1], w_pg[0].astype(BF16), w_pp[0].astype(BF16), ln_final.reshape(1, d))
    return out.reshape(b, s, d)
```

```python
import jax
import jax.numpy as jnp
from jax import lax
from jax.experimental import pallas as pl
from jax.experimental.pallas import tpu as pltpu

F32 = jnp.float32
BF16 = jnp.bfloat16

D_MODEL = 1024
GRID_W = 64
NA_WIDTH = 512
HG_WIDTH = 512
NA_HEADS = 8
NA_HEAD_DIM = 64
NA_ROWS = 8
NA_COLS = 16
HG_HEAD_DIM = 128
HG_HEADS = 4
PLE_DIM = 256
NORM_EPS = 1e-6
N_GROUPS = 9
PROJ_TOTAL = N_GROUPS * 512

V7X_LANES = 128
V7X_VMEM_LIMIT = 56 * 1024 * 1024
MASK_VALUE = -1e30
LOG2E = 1.4426950408889634

PROJ_TM = 1024
PROJ_SPLIT = 4
OUT_TM = 1024
OUT_SPLIT = 4
HG_CHUNK = 128
HG_STEP = 2048
HG_BASE = HG_CHUNK
HG_MAX_LOG2_RANGE = 110.0
NA_PAIRS = NA_HEADS // 2
NA_ROWS_PER_ITER = 32


def _sigmoid(x):
    return 1.0 / (1.0 + jnp.exp(-x))


def _proj_kernel(x_ref, g_ref, w_ref, lbl_ref,
                 aq_ref, ak_ref, av_ref, ag_ref, hq_ref,
                 lff_ref, kf_ref, lfb_ref, kb_ref, hv_ref, hg_ref):
    l0 = lbl_ref[0]
    l1 = lbl_ref[1]
    m = jnp.maximum(l0, l1)
    e0 = jnp.exp(l0 - m)
    e1 = jnp.exp(l1 - m)
    lb = e0 / (e0 + e1)

    sub = x_ref.shape[0] // PROJ_SPLIT
    tiles = [slice(u * sub, (u + 1) * sub) for u in range(PROJ_SPLIT)]
    hns = []
    for rows in tiles:
        x = x_ref[rows, :]
        ms = jnp.mean(x * x, axis=-1, keepdims=True)
        hns.append((x * lax.rsqrt(ms + NORM_EPS) * g_ref[...]).astype(BF16))

    def silu(t):
        return t * _sigmoid(t)

    def forget(lbj):
        def act(t):
            f = lbj + (1.0 - lbj) * _sigmoid(t)
            return jnp.log(f), 1.0 - f
        return act

    scale_q = lambda t: t * (NA_HEAD_DIM ** -0.5 * LOG2E)
    ident = lambda t: t
    groups = ((scale_q, (aq_ref,)), (ident, (ak_ref,)), (ident, (av_ref,)), (silu, (ag_ref,)),
              (silu, (hq_ref,)), (forget(lb[0:1, :]), (lff_ref, kf_ref)),
              (forget(lb[1:2, :]), (lfb_ref, kb_ref)), (ident, (hv_ref,)), (silu, (hg_ref,)))
    for j, (act, out_refs) in enumerate(groups):
        w = w_ref[:, j * 512:(j + 1) * 512]
        for rows, hn in zip(tiles, hns):
            res = act(jnp.dot(hn, w, preferred_element_type=F32))
            for ref, val in zip(out_refs, res if isinstance(res, tuple) else (res,)):
                ref[rows, :] = val.astype(BF16)


def _proj(x2, ln_mix, w_in, lb_logits):
    n = x2.shape[0]
    out_sds = jax.ShapeDtypeStruct((n, 512), BF16)
    row_spec = pl.BlockSpec((PROJ_TM, 512), lambda i: (i, 0))
    return pl.pallas_call(
        _proj_kernel,
        grid=(n // PROJ_TM,),
        in_specs=[
            pl.BlockSpec((PROJ_TM, D_MODEL), lambda i: (i, 0)),
            pl.BlockSpec((1, D_MODEL), lambda i: (0, 0)),
            pl.BlockSpec((D_MODEL, PROJ_TOTAL), lambda i: (0, 0), pipeline_mode=pl.Buffered(1)),
            pl.BlockSpec((2, 2, HG_WIDTH), lambda i: (0, 0, 0)),
        ],
        out_specs=[row_spec] * 11,
        out_shape=[out_sds] * 11,
        compiler_params=pltpu.CompilerParams(
            dimension_semantics=("parallel",), vmem_limit_bytes=V7X_VMEM_LIMIT),
        name="proj",
    )(x2, ln_mix, w_in, lb_logits)


def _attn_build_table(rpb_ref, tab_ref, pair):
    kh = tab_ref.shape[0]
    n_dr, n_dc = 2 * NA_ROWS - 1, 2 * NA_COLS - 1
    c_idx = lax.broadcasted_iota(jnp.int32, (GRID_W, V7X_LANES), 0)
    kc_idx = lax.broadcasted_iota(jnp.int32, (GRID_W, V7X_LANES), 1) % GRID_W
    start = jnp.clip(c_idx - NA_COLS // 2, 0, GRID_W - NA_COLS)
    valid = (kc_idx >= start) & (kc_idx < start + NA_COLS)
    diff = kc_idx - c_idx + NA_COLS - 1
    low_half = lax.broadcasted_iota(jnp.int32, (GRID_W, V7X_LANES), 1) < GRID_W
    for j in range(2):
        head = 2 * pair + j
        toeplitz = {}
        for dr in range(NA_ROWS - kh, NA_ROWS - 1 + kh):
            t = jnp.full((GRID_W, V7X_LANES), MASK_VALUE, F32)
            for dc in range(n_dc):
                t = jnp.where(diff == dc, rpb_ref[(head * n_dr + dr) * n_dc + dc] * LOG2E, t)
            toeplitz[dr] = jnp.where(valid, t, MASK_VALUE)
        for d in range(kh):
            for i in range(0, kh, 2):
                blk = jnp.where(low_half, toeplitz[i - d + NA_ROWS - 1], toeplitz[i + 1 - d + NA_ROWS - 1])
                tab_ref[d, j * GRID_W:(j + 1) * GRID_W, i * GRID_W:(i + 2) * GRID_W] = blk


def _attn_kernel(rpb_ref, q_ref, k_ref, v_ref, sg_ref, g_ref, o_ref, tab_ref):
    rows = q_ref.shape[1] // GRID_W
    kh = min(NA_ROWS, rows)
    nkeys = kh * GRID_W
    lane = lax.broadcasted_iota(jnp.int32, (GRID_W, V7X_LANES), 1)
    first = lane < NA_HEAD_DIM
    m0 = first.astype(F32).astype(BF16)
    m1 = (1.0 - first.astype(F32)).astype(BF16)
    gain = g_ref[...]

    @pl.when(pl.program_id(1) == 0)
    def _():
        _attn_build_table(rpb_ref, tab_ref, pl.program_id(0))

    def scores(r):
        rs = jnp.clip(r - kh // 2, 0, rows - kh)
        q0 = pl.multiple_of(r * GRID_W, GRID_W)
        k0 = pl.multiple_of(rs * GRID_W, GRID_W)
        q = q_ref[0, pl.ds(q0, GRID_W), :]
        kk = k_ref[0, pl.ds(k0, nkeys), :]
        qm = jnp.concatenate([m0 * q, m1 * q], axis=0)
        s = lax.dot_general(qm, kk, (((1,), (1,)), ((), ())), preferred_element_type=F32)
        return q0, k0, s + tab_ref[r - rs]

    def softmax(s):
        p = jnp.exp2(s - jnp.max(s, axis=-1, keepdims=True))
        return p.astype(BF16), jnp.sum(p, axis=-1, keepdims=True)

    def finish(q0, o2, den):
        o2 = o2 / den
        o = jnp.where(first, o2[:GRID_W], o2[GRID_W:])
        sq = o * o
        ms0 = jnp.sum(jnp.where(first, sq, 0.0), axis=-1, keepdims=True)
        ms1 = jnp.sum(jnp.where(first, 0.0, sq), axis=-1, keepdims=True)
        ms = jnp.where(first, ms0, ms1) * (1.0 / NA_HEAD_DIM)
        y = o * lax.rsqrt(ms + NORM_EPS) * gain
        y = y * sg_ref[0, pl.ds(q0, GRID_W), :].astype(F32)
        o_ref[0, pl.ds(q0, GRID_W), :] = y.astype(BF16)

    per_iter = min(NA_ROWS_PER_ITER, rows)

    def body(i, carry):
        rs_ = [scores(i * per_iter + u) for u in range(per_iter)]
        ps = [softmax(s) for _, _, s in rs_]
        os_ = [jnp.dot(p, v_ref[0, pl.ds(k0, nkeys), :], preferred_element_type=F32)
               for (_, k0, _), (p, _) in zip(rs_, ps)]
        for (q0, _, _), (_, den), o2 in zip(rs_, ps, os_):
            finish(q0, o2, den)
        return carry

    lax.fori_loop(0, rows // per_iter, body, 0)


def _attn(rpb_flat, aq, ak, av, ag, attn_norm):
    b, s, _ = aq.shape
    kh = min(NA_ROWS, s // GRID_W)
    assert kh % 2 == 0
    blk = pl.BlockSpec((1, s, V7X_LANES), lambda hp, i: (i, 0, hp))
    return pl.pallas_call(
        _attn_kernel,
        grid=(NA_PAIRS, b),
        in_specs=[pl.BlockSpec(memory_space=pltpu.SMEM), blk, blk, blk, blk,
                  pl.BlockSpec((1, V7X_LANES), lambda hp, i: (0, hp))],
        out_specs=blk,
        out_shape=jax.ShapeDtypeStruct((b, s, NA_WIDTH), BF16),
        scratch_shapes=[pltpu.VMEM((kh, 2 * GRID_W, kh * GRID_W), F32)],
        compiler_params=pltpu.CompilerParams(
            dimension_semantics=("parallel", "arbitrary"), vmem_limit_bytes=V7X_VMEM_LIMIT),
        name="attn",
    )(rpb_flat, aq, ak, av, ag, attn_norm)


def _block_ref(b, n, rev):
    c, w = b.shape
    pos = n if rev else n - 1
    if 2 * n >= 8:
        b3 = b.reshape(c // (2 * n), 2 * n, w)
        return jnp.broadcast_to(b3[:, pos:pos + 1, :], b3.shape).reshape(c, w)
    row = lax.broadcasted_iota(jnp.int32, b.shape, 0)
    if n == 1:
        if rev:
            return jnp.where(row % 2 == 0, pltpu.roll(b, c - 1, 0), b)
        return jnp.where(row % 2 == 1, pltpu.roll(b, 1, 0), b)
    assert n == 2
    if rev:
        c1 = jnp.where(row % 2 == 1, pltpu.roll(b, 1, 0), b)
        return jnp.where(row % 4 < 2, pltpu.roll(c1, c - 2, 0), c1)
    c1 = jnp.where(row % 2 == 0, pltpu.roll(b, c - 1, 0), b)
    return jnp.where(row % 4 >= 2, pltpu.roll(c1, 2, 0), c1)


def _level_decay(b, n, rev):
    c, w = b.shape
    if n < 8:
        return jnp.exp2(-jnp.abs(b - _block_ref(b, n, rev)))
    b3 = b.reshape(c // (2 * n), 2 * n, w)
    lo, hi = b3[:, :n], b3[:, n:]
    if rev:
        ref = b3[:, n:n + 1]
        parts = [jnp.exp2(lo - ref), jnp.exp2(ref - hi)]
    else:
        ref = b3[:, n - 1:n]
        parts = [jnp.exp2(ref - lo), jnp.exp2(hi - ref)]
    return jnp.concatenate(parts, axis=1).reshape(c, w)


def _base_ref(b3, rev):
    base = b3.shape[1]
    pos = base // 2 if rev else base // 2 - 1
    return b3[:, pos:pos + 1]


def _level_ids(c, rev, base):
    t_idx = lax.broadcasted_iota(jnp.int32, (c, c), 0)
    s_idx = lax.broadcasted_iota(jnp.int32, (c, c), 1)
    x = t_idx ^ s_idx
    lev = jnp.ones((c, c), jnp.int32)
    n = 2 * base
    while n < c:
        lev = lev + (x >= n).astype(jnp.int32)
        n *= 2
    before = (t_idx < s_idx) if rev else (t_idx > s_idx)
    lev = jnp.where(before & (x >= base), lev, -1)
    return jnp.where((x < base) & (before | (t_idx == s_idx)), 0, lev)


def _hgrn_scores(q, kk, b, lev, rev, base):
    c, w = q.shape
    nt = (((1,), (1,)), ((), ()))

    if base == 1:
        scores = lax.dot_general(q, kk, nt, preferred_element_type=F32)
    else:
        b3 = b.reshape(c // base, base, w)
        d = (b3 - _base_ref(b3, rev)).reshape(c, w)
        scores = lax.dot_general(jnp.exp2(d).astype(BF16) * q, jnp.exp2(-d).astype(BF16) * kk, nt,
                                 preferred_element_type=F32)
    scores = jnp.where(lev == 0, scores, 0.0)
    n, li = base, 1
    while n < c:
        e = _level_decay(b, n, rev).astype(BF16)
        sl = lax.dot_general(e * q, e * kk, nt, preferred_element_type=F32)
        scores = jnp.where(lev == li, sl, scores)
        n, li = 2 * n, li + 1
    return scores.astype(BF16)


def _hgrn_output(scores, q, kk, v, b, st, rev):
    c, w = q.shape
    btot = b[0:1, :] if rev else b[c - 1:c, :]
    lhs = jnp.concatenate([scores, jnp.exp2(b).astype(BF16) * q], axis=1)
    rhs = jnp.concatenate([v, st.astype(BF16)], axis=0)
    o = jnp.dot(lhs, rhs, preferred_element_type=F32)

    kd = jnp.exp2(btot - b).astype(BF16) * kk
    upd = lax.dot_general(kd, v, (((0,), (0,)), ((), ())), preferred_element_type=F32)
    decay = jnp.broadcast_to(jnp.exp2(btot), (w, w)).T
    return o, st * decay + upd


def _hgrn_kernel(qf_ref, vf_ref, lff_ref, kf_ref, qb_ref, vb_ref, lfb_ref, kb_ref,
                 of_ref, ob_ref, st_ref, b_ref):
    @pl.when(pl.program_id(1) == 0)
    def _():
        st_ref[...] = jnp.zeros_like(st_ref)

    c = HG_CHUNK
    n_sub = qf_ref.shape[1] // c
    t_idx = lax.broadcasted_iota(jnp.int32, (c, c), 0)
    u_idx = lax.broadcasted_iota(jnp.int32, (c, c), 1)
    dirs = ((0, False, qf_ref, kf_ref, vf_ref, lff_ref, of_ref),
            (1, True, qb_ref, kb_ref, vb_ref, lfb_ref, ob_ref))

    widest = jnp.zeros((c // HG_BASE, 1, lff_ref.shape[2]), F32)
    for di, rev, _, _, _, lf_ref, _ in dirs:
        tri = ((u_idx >= t_idx) if rev else (u_idx <= t_idx)).astype(F32).astype(BF16)
        for j in range(n_sub):
            rows = slice(j * c, (j + 1) * c)
            bj = jnp.dot(tri, lf_ref[0, rows, :], preferred_element_type=F32) * LOG2E
            b_ref[di, rows, :] = bj
            b3 = bj.reshape(c // HG_BASE, HG_BASE, bj.shape[1])
            ref = _base_ref(b3, rev)
            widest = jnp.maximum(widest, jnp.maximum(jnp.abs(b3[:, 0:1] - ref),
                                                     jnp.abs(b3[:, HG_BASE - 1:HG_BASE] - ref)))
    fast = jnp.max(widest) <= HG_MAX_LOG2_RANGE

    def run(base):
        lev = {rev: _level_ids(c, rev, base) for rev in (False, True)}

        def body(j, carry):
            work = []
            for di, rev, q_ref, k_ref, v_ref, _, o_ref in dirs:
                r0 = ((n_sub - 1 - j) if rev else j) * c
                if not isinstance(j, int):
                    r0 = pl.multiple_of(r0, c)
                for h in range(HG_HEADS):
                    cs = slice(h * HG_HEAD_DIM, (h + 1) * HG_HEAD_DIM)
                    ld = lambda ref: ref[0, pl.ds(r0, c), cs]
                    q, kk, bb = ld(q_ref), ld(k_ref), b_ref[di, pl.ds(r0, c), cs]
                    scores = _hgrn_scores(q, kk, bb, lev[rev], rev, base)
                    work.append((di, rev, h, r0, cs, q, kk, ld(v_ref), bb, scores, o_ref))
            for di, rev, h, r0, cs, q, kk, v, bb, scores, o_ref in work:
                o, st = _hgrn_output(scores, q, kk, v, bb, st_ref[di, h], rev)
                st_ref[di, h] = st
                o_ref[0, pl.ds(r0, c), cs] = o.astype(o_ref.dtype)
            return carry

        if base == 1:
            lax.fori_loop(0, n_sub, body, 0)
        else:
            for j in range(n_sub):
                body(j, 0)

    pl.when(fast)(lambda: run(HG_BASE))
    pl.when(jnp.logical_not(fast))(lambda: run(1))


def _hgrn(hq, hv, lff, kf, lfb, kb):
    b, s, w = hq.shape
    ns = s // HG_STEP
    fwd = pl.BlockSpec((1, HG_STEP, w), lambda i, j: (i, j, 0))
    bwd = pl.BlockSpec((1, HG_STEP, w), lambda i, j: (i, ns - 1 - j, 0))
    sds = jax.ShapeDtypeStruct((b, s, w), BF16)
    return pl.pallas_call(
        _hgrn_kernel,
        grid=(b, ns),
        in_specs=[fwd, fwd, fwd, fwd, bwd, bwd, bwd, bwd],
        out_specs=[fwd, bwd],
        out_shape=[sds, sds],
        scratch_shapes=[pltpu.VMEM((2, HG_HEADS, HG_HEAD_DIM, HG_HEAD_DIM), F32),
                        pltpu.VMEM((2, HG_STEP, w), F32)],
        compiler_params=pltpu.CompilerParams(
            dimension_semantics=("parallel", "arbitrary"), vmem_limit_bytes=V7X_VMEM_LIMIT),
        name="hgrn",
    )(hq, hv, lff, kf, hq, hv, lfb, kb)


def _rms(x, g):
    ms = jnp.mean(x * x, axis=-1, keepdims=True)
    return x * lax.rsqrt(ms + NORM_EPS) * g


def _out_kernel(a_ref, of_ref, ob_ref, hg_ref, x_ref, p_ref,
                gh_ref, wo_ref, gp_ref, wg_ref, wp_ref, gf_ref, o_ref):
    sub = x_ref.shape[0] // OUT_SPLIT
    tiles = [slice(u * sub, (u + 1) * sub) for u in range(OUT_SPLIT)]
    gh = gh_ref[...]

    def mixed(rows):
        r = of_ref[rows, :].astype(F32) + ob_ref[rows, :].astype(F32)
        parts = [a_ref[rows, :]]
        for h in range(HG_HEADS):
            cs = slice(h * HG_HEAD_DIM, (h + 1) * HG_HEAD_DIM)
            y = _rms(r[:, cs], gh[:, cs]) * hg_ref[rows, cs].astype(F32)
            parts.append(y.astype(BF16))
        return jnp.concatenate(parts, axis=-1)

    cats = [mixed(rows) for rows in tiles]
    pps = [jnp.dot(p_ref[rows, :].astype(BF16), wp_ref[...], preferred_element_type=F32)
           for rows in tiles]
    h1s = [x_ref[rows, :] + jnp.dot(cat, wo_ref[...], preferred_element_type=F32)
           for rows, cat in zip(tiles, cats)]
    gates = [_sigmoid(jnp.dot(_rms(h1, gp_ref[...]).astype(BF16), wg_ref[...],
                              preferred_element_type=F32)) for h1 in h1s]
    for rows, h1, gate, pp in zip(tiles, h1s, gates, pps):
        o_ref[rows, :] = _rms(h1 + gate * pp, gf_ref[...])


def _out(a, o_f, o_b, hg, x2, p2, hgrn_norm, w_out, ln_ple, w_pg, w_pp, ln_final):
    n = x2.shape[0]
    row = lambda w: pl.BlockSpec((OUT_TM, w), lambda i: (i, 0))
    full = lambda r, c: pl.BlockSpec((r, c), lambda i: (0, 0))
    return pl.pallas_call(
        _out_kernel,
        grid=(n // OUT_TM,),
        in_specs=[row(NA_WIDTH), row(HG_WIDTH), row(HG_WIDTH), row(HG_WIDTH),
                  row(D_MODEL), row(PLE_DIM),
                  full(1, HG_WIDTH), full(D_MODEL, D_MODEL), full(1, D_MODEL),
                  full(D_MODEL, D_MODEL), full(PLE_DIM, D_MODEL), full(1, D_MODEL)],
        out_specs=row(D_MODEL),
        out_shape=jax.ShapeDtypeStruct((n, D_MODEL), F32),
        compiler_params=pltpu.CompilerParams(
            dimension_semantics=("parallel",), vmem_limit_bytes=V7X_VMEM_LIMIT),
        name="out",
    )(a, o_f, o_b, hg, x2, p2, hgrn_norm, w_out, ln_ple, w_pg, w_pp, ln_final)


def kernel(x, p, ln_mix, w_in, rpb, lb_logits, attn_norm, hgrn_norm, w_out, ln_ple, w_pg, w_pp, ln_final):
    depth = p.shape[0]
    assert depth == 1 and lb_logits.shape[0] == 2
    b, s, d = x.shape
    n = b * s
    rows = s // GRID_W
    assert d == D_MODEL and n % PROJ_TM == 0 and n % OUT_TM == 0 and s % HG_STEP == 0
    assert rows % min(NA_ROWS_PER_ITER, rows) == 0 and rows >= NA_ROWS

    x2 = x.reshape(n, d)
    (aq, ak, av, ag, hq, lff, kf, lfb, kb, hv, hg) = _proj(
        x2, ln_mix[0:1], w_in[0].astype(BF16), lb_logits)

    r3 = lambda t: t.reshape(b, s, t.shape[-1])
    a = _attn(rpb[0].reshape(-1), r3(aq), r3(ak), r3(av), r3(ag), attn_norm[0:1])
    o_f, o_b = _hgrn(r3(hq), r3(hv), r3(lff), r3(kf), r3(lfb), r3(kb))

    out = _out(a.reshape(n, NA_WIDTH), o_f.reshape(n, HG_WIDTH), o_b.reshape(n, HG_WIDTH), hg,
               x2, p[0].reshape(n, PLE_DIM), hgrn_norm[0:1], w_out[0].astype(BF16),
               ln_ple[0:1], w_pg[0].astype(BF16), w_pp[0].astype(BF16), ln_final.reshape(1, d))
    return out.reshape(b, s, d)
```

```python
import jax
import jax.numpy as jnp
from jax import lax
from jax.experimental import pallas as pl
from jax.experimental.pallas import tpu as pltpu

F32 = jnp.float32
BF16 = jnp.bfloat16

D_MODEL = 1024
GRID_W = 64
NA_WIDTH = 512
HG_WIDTH = 512
NA_HEADS = 8
NA_HEAD_DIM = 64
NA_ROWS = 8
NA_COLS = 16
HG_HEAD_DIM = 128
HG_HEADS = 4
PLE_DIM = 256
NORM_EPS = 1e-6
N_GROUPS = 9
PROJ_TOTAL = N_GROUPS * 512

V7X_LANES = 128
V7X_VMEM_LIMIT = 56 * 1024 * 1024
MASK_VALUE = -1e30
LOG2E = 1.4426950408889634

PROJ_TM = 1024
PROJ_SPLIT = 4
OUT_TM = 1024
OUT_SPLIT = 4
HG_CHUNK = 128
HG_STEP = 2048
HG_BASE = HG_CHUNK
HG_MAX_LOG2_RANGE = 110.0
NA_PAIRS = NA_HEADS // 2
NA_ROWS_PER_ITER = 64


def _sigmoid(x):
    return 1.0 / (1.0 + jnp.exp(-x))


def _proj_kernel(x_ref, g_ref, w_ref, lbl_ref,
                 aq_ref, ak_ref, av_ref, ag_ref, hq_ref,
                 lff_ref, kf_ref, lfb_ref, kb_ref, hv_ref, hg_ref):
    l0 = lbl_ref[0]
    l1 = lbl_ref[1]
    m = jnp.maximum(l0, l1)
    e0 = jnp.exp(l0 - m)
    e1 = jnp.exp(l1 - m)
    lb = e0 / (e0 + e1)

    sub = x_ref.shape[0] // PROJ_SPLIT
    tiles = [slice(u * sub, (u + 1) * sub) for u in range(PROJ_SPLIT)]
    hns = []
    for rows in tiles:
        x = x_ref[rows, :]
        ms = jnp.mean(x * x, axis=-1, keepdims=True)
        hns.append((x * lax.rsqrt(ms + NORM_EPS) * g_ref[...]).astype(BF16))

    def silu(t):
        return t * _sigmoid(t)

    def forget(lbj):
        def act(t):
            f = lbj + (1.0 - lbj) * _sigmoid(t)
            return jnp.log(f), 1.0 - f
        return act

    scale_q = lambda t: t * (NA_HEAD_DIM ** -0.5 * LOG2E)
    ident = lambda t: t
    groups = ((scale_q, (aq_ref,)), (ident, (ak_ref,)), (ident, (av_ref,)), (silu, (ag_ref,)),
              (silu, (hq_ref,)), (forget(lb[0:1, :]), (lff_ref, kf_ref)),
              (forget(lb[1:2, :]), (lfb_ref, kb_ref)), (ident, (hv_ref,)), (silu, (hg_ref,)))
    for j, (act, out_refs) in enumerate(groups):
        w = w_ref[:, j * 512:(j + 1) * 512]
        for rows, hn in zip(tiles, hns):
            res = act(jnp.dot(hn, w, preferred_element_type=F32))
            for ref, val in zip(out_refs, res if isinstance(res, tuple) else (res,)):
                ref[rows, :] = val.astype(BF16)


def _proj(x2, ln_mix, w_in, lb_logits):
    n = x2.shape[0]
    out_sds = jax.ShapeDtypeStruct((n, 512), BF16)
    row_spec = pl.BlockSpec((PROJ_TM, 512), lambda i: (i, 0))
    return pl.pallas_call(
        _proj_kernel,
        grid=(n // PROJ_TM,),
        in_specs=[
            pl.BlockSpec((PROJ_TM, D_MODEL), lambda i: (i, 0)),
            pl.BlockSpec((1, D_MODEL), lambda i: (0, 0)),
            pl.BlockSpec((D_MODEL, PROJ_TOTAL), lambda i: (0, 0), pipeline_mode=pl.Buffered(1)),
            pl.BlockSpec((2, 2, HG_WIDTH), lambda i: (0, 0, 0)),
        ],
        out_specs=[row_spec] * 11,
        out_shape=[out_sds] * 11,
        compiler_params=pltpu.CompilerParams(
            dimension_semantics=("parallel",), vmem_limit_bytes=V7X_VMEM_LIMIT),
        name="proj",
    )(x2, ln_mix, w_in, lb_logits)


def _attn_build_table(rpb_ref, tab_ref, pair):
    kh = tab_ref.shape[0]
    n_dr, n_dc = 2 * NA_ROWS - 1, 2 * NA_COLS - 1
    c_idx = lax.broadcasted_iota(jnp.int32, (GRID_W, V7X_LANES), 0)
    kc_idx = lax.broadcasted_iota(jnp.int32, (GRID_W, V7X_LANES), 1) % GRID_W
    start = jnp.clip(c_idx - NA_COLS // 2, 0, GRID_W - NA_COLS)
    valid = (kc_idx >= start) & (kc_idx < start + NA_COLS)
    diff = kc_idx - c_idx + NA_COLS - 1
    low_half = lax.broadcasted_iota(jnp.int32, (GRID_W, V7X_LANES), 1) < GRID_W
    for j in range(2):
        head = 2 * pair + j
        toeplitz = {}
        for dr in range(NA_ROWS - kh, NA_ROWS - 1 + kh):
            t = jnp.full((GRID_W, V7X_LANES), MASK_VALUE, F32)
            for dc in range(n_dc):
                t = jnp.where(diff == dc, rpb_ref[(head * n_dr + dr) * n_dc + dc] * LOG2E, t)
            toeplitz[dr] = jnp.where(valid, t, MASK_VALUE)
        for d in range(kh):
            for i in range(0, kh, 2):
                blk = jnp.where(low_half, toeplitz[i - d + NA_ROWS - 1], toeplitz[i + 1 - d + NA_ROWS - 1])
                tab_ref[d, j * GRID_W:(j + 1) * GRID_W, i * GRID_W:(i + 2) * GRID_W] = blk


def _attn_kernel(rpb_ref, q_ref, k_ref, v_ref, sg_ref, g_ref, o_ref, tab_ref):
    rows = q_ref.shape[1] // GRID_W
    kh = min(NA_ROWS, rows)
    nkeys = kh * GRID_W
    lane = lax.broadcasted_iota(jnp.int32, (GRID_W, V7X_LANES), 1)
    first = lane < NA_HEAD_DIM
    m0 = first.astype(F32).astype(BF16)
    m1 = (1.0 - first.astype(F32)).astype(BF16)
    gain = g_ref[...]

    @pl.when(pl.program_id(1) == 0)
    def _():
        _attn_build_table(rpb_ref, tab_ref, pl.program_id(0))

    def scores(r):
        rs = jnp.clip(r - kh // 2, 0, rows - kh)
        q0 = pl.multiple_of(r * GRID_W, GRID_W)
        k0 = pl.multiple_of(rs * GRID_W, GRID_W)
        q = q_ref[0, pl.ds(q0, GRID_W), :]
        kk = k_ref[0, pl.ds(k0, nkeys), :]
        qm = jnp.concatenate([m0 * q, m1 * q], axis=0)
        s = lax.dot_general(qm, kk, (((1,), (1,)), ((), ())), preferred_element_type=F32)
        return q0, k0, s + tab_ref[r - rs]

    def softmax(s):
        p = jnp.exp2(s - jnp.max(s, axis=-1, keepdims=True))
        return p.astype(BF16), jnp.sum(p, axis=-1, keepdims=True)

    def finish(q0, o2, den):
        o2 = o2 / den
        o = jnp.where(first, o2[:GRID_W], o2[GRID_W:])
        sq = o * o
        ms0 = jnp.sum(jnp.where(first, sq, 0.0), axis=-1, keepdims=True)
        ms1 = jnp.sum(jnp.where(first, 0.0, sq), axis=-1, keepdims=True)
        ms = jnp.where(first, ms0, ms1) * (1.0 / NA_HEAD_DIM)
        y = o * lax.rsqrt(ms + NORM_EPS) * gain
        y = y * sg_ref[0, pl.ds(q0, GRID_W), :].astype(F32)
        o_ref[0, pl.ds(q0, GRID_W), :] = y.astype(BF16)

    per_iter = min(NA_ROWS_PER_ITER, rows)

    def body(i, carry):
        rs_ = [scores(i * per_iter + u) for u in range(per_iter)]
        ps = [softmax(s) for _, _, s in rs_]
        os_ = [jnp.dot(p, v_ref[0, pl.ds(k0, nkeys), :], preferred_element_type=F32)
               for (_, k0, _), (p, _) in zip(rs_, ps)]
        for (q0, _, _), (_, den), o2 in zip(rs_, ps, os_):
            finish(q0, o2, den)
        return carry

    lax.fori_loop(0, rows // per_iter, body, 0)


def _attn(rpb_flat, aq, ak, av, ag, attn_norm):
    b, s, _ = aq.shape
    kh = min(NA_ROWS, s // GRID_W)
    assert kh % 2 == 0
    blk = pl.BlockSpec((1, s, V7X_LANES), lambda hp, i: (i, 0, hp))
    return pl.pallas_call(
        _attn_kernel,
        grid=(NA_PAIRS, b),
        in_specs=[pl.BlockSpec(memory_space=pltpu.SMEM), blk, blk, blk, blk,
                  pl.BlockSpec((1, V7X_LANES), lambda hp, i: (0, hp))],
        out_specs=blk,
        out_shape=jax.ShapeDtypeStruct((b, s, NA_WIDTH), BF16),
        scratch_shapes=[pltpu.VMEM((kh, 2 * GRID_W, kh * GRID_W), F32)],
        compiler_params=pltpu.CompilerParams(
            dimension_semantics=("parallel", "arbitrary"), vmem_limit_bytes=V7X_VMEM_LIMIT),
        name="attn",
    )(rpb_flat, aq, ak, av, ag, attn_norm)


def _block_ref(b, n, rev):
    c, w = b.shape
    pos = n if rev else n - 1
    if 2 * n >= 8:
        b3 = b.reshape(c // (2 * n), 2 * n, w)
        return jnp.broadcast_to(b3[:, pos:pos + 1, :], b3.shape).reshape(c, w)
    row = lax.broadcasted_iota(jnp.int32, b.shape, 0)
    if n == 1:
        if rev:
            return jnp.where(row % 2 == 0, pltpu.roll(b, c - 1, 0), b)
        return jnp.where(row % 2 == 1, pltpu.roll(b, 1, 0), b)
    assert n == 2
    if rev:
        c1 = jnp.where(row % 2 == 1, pltpu.roll(b, 1, 0), b)
        return jnp.where(row % 4 < 2, pltpu.roll(c1, c - 2, 0), c1)
    c1 = jnp.where(row % 2 == 0, pltpu.roll(b, c - 1, 0), b)
    return jnp.where(row % 4 >= 2, pltpu.roll(c1, 2, 0), c1)


def _level_decay(b, n, rev):
    c, w = b.shape
    if n < 8:
        return jnp.exp2(-jnp.abs(b - _block_ref(b, n, rev)))
    b3 = b.reshape(c // (2 * n), 2 * n, w)
    lo, hi = b3[:, :n], b3[:, n:]
    if rev:
        ref = b3[:, n:n + 1]
        parts = [jnp.exp2(lo - ref), jnp.exp2(ref - hi)]
    else:
        ref = b3[:, n - 1:n]
        parts = [jnp.exp2(ref - lo), jnp.exp2(hi - ref)]
    return jnp.concatenate(parts, axis=1).reshape(c, w)


def _base_ref(b3, rev):
    base = b3.shape[1]
    pos = base // 2 if rev else base // 2 - 1
    return b3[:, pos:pos + 1]


def _level_ids(c, rev, base):
    t_idx = lax.broadcasted_iota(jnp.int32, (c, c), 0)
    s_idx = lax.broadcasted_iota(jnp.int32, (c, c), 1)
    x = t_idx ^ s_idx
    lev = jnp.ones((c, c), jnp.int32)
    n = 2 * base
    while n < c:
        lev = lev + (x >= n).astype(jnp.int32)
        n *= 2
    before = (t_idx < s_idx) if rev else (t_idx > s_idx)
    lev = jnp.where(before & (x >= base), lev, -1)
    return jnp.where((x < base) & (before | (t_idx == s_idx)), 0, lev)


def _two_head_dot(qs, ks):
    w = HG_HEAD_DIM
    zero = jnp.zeros_like(ks[:, :w])
    rhs = jnp.concatenate([jnp.concatenate([ks[:, :w], zero], axis=1),
                           jnp.concatenate([zero, ks[:, w:]], axis=1)], axis=0)
    return lax.dot_general(qs, rhs, (((1,), (1,)), ((), ())), preferred_element_type=F32)


def _hgrn_scores(q, kk, b, lev, rev, base):
    c, w = q.shape

    if base == 1:
        scores, growth, decay = _two_head_dot(q, kk), None, None
    else:
        b3 = b.reshape(c // base, base, w)
        d = (b3 - _base_ref(b3, rev)).reshape(c, w)
        growth, decay = jnp.exp2(-d).astype(BF16), jnp.exp2(d).astype(BF16)
        scores = _two_head_dot(decay * q, growth * kk)
    scores = jnp.where(lev == 0, scores, 0.0)
    n, li = base, 1
    while n < c:
        e = _level_decay(b, n, rev).astype(BF16)
        scores = jnp.where(lev == li, _two_head_dot(e * q, e * kk), scores)
        n, li = 2 * n, li + 1
    return scores.astype(BF16), growth, decay


def _hgrn_output(scores, q, kk, v, b, st, rev, growth, decay):
    c, w = q.shape
    btot = b[0:1, :] if rev else b[c - 1:c, :]
    if decay is None:
        from_start, to_end = jnp.exp2(b).astype(BF16), jnp.exp2(btot - b).astype(BF16)
    else:
        bmid = _base_ref(b.reshape(1, c, w), rev).reshape(1, w)
        from_start = decay * jnp.exp2(bmid).astype(BF16)
        to_end = growth * jnp.exp2(btot - bmid).astype(BF16)
    lhs = jnp.concatenate([scores, from_start * q], axis=1)
    rhs = jnp.concatenate([v, st.astype(BF16)], axis=0)
    o = jnp.dot(lhs, rhs, preferred_element_type=F32)

    kd = to_end * kk
    upd = lax.dot_general(kd, v, (((0,), (0,)), ((), ())), preferred_element_type=F32)
    keep = jnp.broadcast_to(jnp.exp2(btot), (w, w)).T
    return o, st * keep + upd


def _hgrn_kernel(qf_ref, vf_ref, lff_ref, kf_ref, qb_ref, vb_ref, lfb_ref, kb_ref,
                 of_ref, ob_ref, st_ref, b_ref):
    @pl.when(pl.program_id(1) == 0)
    def _():
        st_ref[...] = jnp.zeros_like(st_ref)

    c = HG_CHUNK
    n_sub = qf_ref.shape[1] // c
    t_idx = lax.broadcasted_iota(jnp.int32, (c, c), 0)
    u_idx = lax.broadcasted_iota(jnp.int32, (c, c), 1)
    dirs = ((0, False, qf_ref, kf_ref, vf_ref, lff_ref, of_ref),
            (1, True, qb_ref, kb_ref, vb_ref, lfb_ref, ob_ref))

    widest = jnp.zeros((c // HG_BASE, 1, lff_ref.shape[2]), F32)
    for di, rev, _, _, _, lf_ref, _ in dirs:
        tri = ((u_idx >= t_idx) if rev else (u_idx <= t_idx)).astype(F32).astype(BF16)
        for j in range(n_sub):
            rows = slice(j * c, (j + 1) * c)
            bj = jnp.dot(tri, lf_ref[0, rows, :], preferred_element_type=F32) * LOG2E
            b_ref[di, rows, :] = bj
            b3 = bj.reshape(c // HG_BASE, HG_BASE, bj.shape[1])
            ref = _base_ref(b3, rev)
            widest = jnp.maximum(widest, jnp.maximum(jnp.abs(b3[:, 0:1] - ref),
                                                     jnp.abs(b3[:, HG_BASE - 1:HG_BASE] - ref)))
    fast = jnp.max(widest) <= HG_MAX_LOG2_RANGE

    def run(base):
        lev = {rev: jnp.tile(_level_ids(c, rev, base), (1, 2)) for rev in (False, True)}
        hd = HG_HEAD_DIM

        def body(j, carry):
            work = []
            for di, rev, q_ref, k_ref, v_ref, _, o_ref in dirs:
                r0 = ((n_sub - 1 - j) if rev else j) * c
                if not isinstance(j, int):
                    r0 = pl.multiple_of(r0, c)
                for h in range(0, HG_HEADS, 2):
                    cs2 = slice(h * hd, (h + 2) * hd)
                    ld = lambda ref: ref[0, pl.ds(r0, c), cs2]
                    q2, kk2, v2, bb2 = ld(q_ref), ld(k_ref), ld(v_ref), b_ref[di, pl.ds(r0, c), cs2]
                    scores2, growth2, decay2 = _hgrn_scores(q2, kk2, bb2, lev[rev], rev, base)
                    for u in range(2):
                        ls = slice(u * hd, (u + 1) * hd)
                        cut = lambda t: None if t is None else t[:, ls]
                        work.append((di, rev, h + u, r0, q2[:, ls], kk2[:, ls], v2[:, ls], bb2[:, ls],
                                     scores2[:, u * c:(u + 1) * c], cut(growth2), cut(decay2), o_ref))
            for di, rev, h, r0, q, kk, v, bb, scores, growth, decay, o_ref in work:
                o, st = _hgrn_output(scores, q, kk, v, bb, st_ref[di, h], rev, growth, decay)
                st_ref[di, h] = st
                o_ref[0, pl.ds(r0, c), h * hd:(h + 1) * hd] = o.astype(o_ref.dtype)
            return carry

        if base == 1:
            lax.fori_loop(0, n_sub, body, 0)
        else:
            for j in range(n_sub):
                body(j, 0)

    pl.when(fast)(lambda: run(HG_BASE))
    pl.when(jnp.logical_not(fast))(lambda: run(1))


def _hgrn(hq, hv, lff, kf, lfb, kb):
    b, s, w = hq.shape
    ns = s // HG_STEP
    fwd = pl.BlockSpec((1, HG_STEP, w), lambda i, j: (i, j, 0))
    bwd = pl.BlockSpec((1, HG_STEP, w), lambda i, j: (i, ns - 1 - j, 0))
    sds = jax.ShapeDtypeStruct((b, s, w), BF16)
    return pl.pallas_call(
        _hgrn_kernel,
        grid=(b, ns),
        in_specs=[fwd, fwd, fwd, fwd, bwd, bwd, bwd, bwd],
        out_specs=[fwd, bwd],
        out_shape=[sds, sds],
        scratch_shapes=[pltpu.VMEM((2, HG_HEADS, HG_HEAD_DIM, HG_HEAD_DIM), F32),
                        pltpu.VMEM((2, HG_STEP, w), F32)],
        compiler_params=pltpu.CompilerParams(
            dimension_semantics=("parallel", "arbitrary"), vmem_limit_bytes=V7X_VMEM_LIMIT),
        name="hgrn",
    )(hq, hv, lff, kf, hq, hv, lfb, kb)


def _rms(x, g):
    ms = jnp.mean(x * x, axis=-1, keepdims=True)
    return x * lax.rsqrt(ms + NORM_EPS) * g


def _out_kernel(a_ref, of_ref, ob_ref, hg_ref, x_ref, p_ref,
                gh_ref, wo_ref, gp_ref, wg_ref, wp_ref, gf_ref, o_ref):
    sub = x_ref.shape[0] // OUT_SPLIT
    tiles = [slice(u * sub, (u + 1) * sub) for u in range(OUT_SPLIT)]
    gh = gh_ref[...]

    def mixed(rows):
        r = of_ref[rows, :].astype(F32) + ob_ref[rows, :].astype(F32)
        parts = [a_ref[rows, :]]
        for h in range(HG_HEADS):
            cs = slice(h * HG_HEAD_DIM, (h + 1) * HG_HEAD_DIM)
            y = _rms(r[:, cs], gh[:, cs]) * hg_ref[rows, cs].astype(F32)
            parts.append(y.astype(BF16))
        return jnp.concatenate(parts, axis=-1)

    cats = [mixed(rows) for rows in tiles]
    pps = [jnp.dot(p_ref[rows, :].astype(BF16), wp_ref[...], preferred_element_type=F32)
           for rows in tiles]
    h1s = [x_ref[rows, :] + jnp.dot(cat, wo_ref[...], preferred_element_type=F32)
           for rows, cat in zip(tiles, cats)]
    gates = [_sigmoid(jnp.dot(_rms(h1, gp_ref[...]).astype(BF16), wg_ref[...],
                              preferred_element_type=F32)) for h1 in h1s]
    for rows, h1, gate, pp in zip(tiles, h1s, gates, pps):
        o_ref[rows, :] = _rms(h1 + gate * pp, gf_ref[...])


def _out(a, o_f, o_b, hg, x2, p2, hgrn_norm, w_out, ln_ple, w_pg, w_pp, ln_final):
    n = x2.shape[0]
    row = lambda w: pl.BlockSpec((OUT_TM, w), lambda i: (i, 0))
    full = lambda r, c: pl.BlockSpec((r, c), lambda i: (0, 0))
    return pl.pallas_call(
        _out_kernel,
        grid=(n // OUT_TM,),
        in_specs=[row(NA_WIDTH), row(HG_WIDTH), row(HG_WIDTH), row(HG_WIDTH),
                  row(D_MODEL), row(PLE_DIM),
                  full(1, HG_WIDTH), full(D_MODEL, D_MODEL), full(1, D_MODEL),
                  full(D_MODEL, D_MODEL), full(PLE_DIM, D_MODEL), full(1, D_MODEL)],
        out_specs=row(D_MODEL),
        out_shape=jax.ShapeDtypeStruct((n, D_MODEL), F32),
        compiler_params=pltpu.CompilerParams(
            dimension_semantics=("parallel",), vmem_limit_bytes=V7X_VMEM_LIMIT),
        name="out",
    )(a, o_f, o_b, hg, x2, p2, hgrn_norm, w_out, ln_ple, w_pg, w_pp, ln_final)


def kernel(x, p, ln_mix, w_in, rpb, lb_logits, attn_norm, hgrn_norm, w_out, ln_ple, w_pg, w_pp, ln_final):
    depth = p.shape[0]
    assert depth == 1 and lb_logits.shape[0] == 2
    b, s, d = x.shape
    n = b * s
    rows = s // GRID_W
    assert d == D_MODEL and n % PROJ_TM == 0 and n % OUT_TM == 0 and s % HG_STEP == 0
    assert rows % min(NA_ROWS_PER_ITER, rows) == 0 and rows >= NA_ROWS

    x2 = x.reshape(n, d)
    (aq, ak, av, ag, hq, lff, kf, lfb, kb, hv, hg) = _proj(
        x2, ln_mix[0:1], w_in[0].astype(BF16), lb_logits)

    r3 = lambda t: t.reshape(b, s, t.shape[-1])
    a = _attn(rpb[0].reshape(-1), r3(aq), r3(ak), r3(av), r3(ag), attn_norm[0:1])
    o_f, o_b = _hgrn(r3(hq), r3(hv), r3(lff), r3(kf), r3(lfb), r3(kb))

    out = _out(a.reshape(n, NA_WIDTH), o_f.reshape(n, HG_WIDTH), o_b.reshape(n, HG_WIDTH), hg,
               x2, p[0].reshape(n, PLE_DIM), hgrn_norm[0:1], w_out[0].astype(BF16),
               ln_ple[0:1], w_pg[0].astype(BF16), w_pp[0].astype(BF16), ln_final.reshape(1, d))
    return out.reshape(b, s, d)
```

```python
import jax
import jax.numpy as jnp
from jax import lax
from jax.experimental import pallas as pl
from jax.experimental.pallas import tpu as pltpu

F32 = jnp.float32
BF16 = jnp.bfloat16

D_MODEL = 1024
GRID_W = 64
NA_WIDTH = 512
HG_WIDTH = 512
NA_HEADS = 8
NA_HEAD_DIM = 64
NA_ROWS = 8
NA_COLS = 16
HG_HEAD_DIM = 128
HG_HEADS = 4
PLE_DIM = 256
NORM_EPS = 1e-6
N_GROUPS = 9
GROUP_WIDTH = NA_WIDTH
assert HG_WIDTH == GROUP_WIDTH
PROJ_TOTAL = N_GROUPS * GROUP_WIDTH

V7X_LANES = 128
V7X_VMEM_LIMIT = 56 * 1024 * 1024
MASK_VALUE = -1e30
LOG2E = 1.4426950408889634

PROJ_TM = 1024
PROJ_SPLIT = 8
OUT_TM = 1024
OUT_SPLIT = 2
HG_CHUNK = 128
HG_STEP = 2048
HG_BASE = HG_CHUNK
HG_MAX_LOG2_RANGE = 110.0
NA_PAIRS = NA_HEADS // 2
NA_ROWS_PER_ITER = 64


def _sigmoid(x):
    return 1.0 / (1.0 + jnp.exp(-x))


def _proj_kernel(x_ref, g_ref, w_ref, lbl_ref,
                 aq_ref, ak_ref, av_ref, ag_ref, hq_ref,
                 lff_ref, lfb_ref, hv_ref, hg_ref):
    l0 = lbl_ref[0]
    l1 = lbl_ref[1]
    m = jnp.maximum(l0, l1)
    e0 = jnp.exp(l0 - m)
    e1 = jnp.exp(l1 - m)
    lb = e0 / (e0 + e1)

    sub = x_ref.shape[0] // PROJ_SPLIT
    tiles = [slice(u * sub, (u + 1) * sub) for u in range(PROJ_SPLIT)]
    hns = []
    for rows in tiles:
        x = x_ref[rows, :]
        ms = jnp.mean(x * x, axis=-1, keepdims=True)
        hns.append((x * lax.rsqrt(ms + NORM_EPS) * g_ref[...]).astype(BF16))

    def silu(t):
        return t * _sigmoid(t)

    def forget(lbj):
        return lambda t: jnp.log(lbj + (1.0 - lbj) * _sigmoid(t))

    scale_q = lambda t: t * (NA_HEAD_DIM ** -0.5 * LOG2E)
    ident = lambda t: t
    groups = ((scale_q, (aq_ref,)), (ident, (ak_ref,)), (ident, (av_ref,)), (silu, (ag_ref,)),
              (silu, (hq_ref,)), (forget(lb[0:1, :]), (lff_ref,)),
              (forget(lb[1:2, :]), (lfb_ref,)), (ident, (hv_ref,)), (silu, (hg_ref,)))
    for j, (act, out_refs) in enumerate(groups):
        w = w_ref[:, j * GROUP_WIDTH:(j + 1) * GROUP_WIDTH]
        for rows, hn in zip(tiles, hns):
            res = act(jnp.dot(hn, w, preferred_element_type=F32))
            for ref in out_refs:
                ref[rows, :] = res.astype(BF16)


def _proj(x2, ln_mix, w_in, lb_logits):
    n = x2.shape[0]
    out_sds = jax.ShapeDtypeStruct((n, GROUP_WIDTH), BF16)
    row_spec = pl.BlockSpec((PROJ_TM, GROUP_WIDTH), lambda i: (i, 0))
    return pl.pallas_call(
        _proj_kernel,
        grid=(n // PROJ_TM,),
        in_specs=[
            pl.BlockSpec((PROJ_TM, D_MODEL), lambda i: (i, 0)),
            pl.BlockSpec((1, D_MODEL), lambda i: (0, 0)),
            pl.BlockSpec((D_MODEL, PROJ_TOTAL), lambda i: (0, 0), pipeline_mode=pl.Buffered(1)),
            pl.BlockSpec((2, 2, HG_WIDTH), lambda i: (0, 0, 0)),
        ],
        out_specs=[row_spec] * 9,
        out_shape=[out_sds] * 9,
        compiler_params=pltpu.CompilerParams(
            dimension_semantics=("parallel",), vmem_limit_bytes=V7X_VMEM_LIMIT),
        name="proj",
    )(x2, ln_mix, w_in, lb_logits)


def _attn_build_table(rpb_ref, tab_ref, pair):
    kh = tab_ref.shape[0]
    n_dr, n_dc = 2 * NA_ROWS - 1, 2 * NA_COLS - 1
    c_idx = lax.broadcasted_iota(jnp.int32, (GRID_W, V7X_LANES), 0)
    kc_idx = lax.broadcasted_iota(jnp.int32, (GRID_W, V7X_LANES), 1) % GRID_W
    start = jnp.clip(c_idx - NA_COLS // 2, 0, GRID_W - NA_COLS)
    valid = (kc_idx >= start) & (kc_idx < start + NA_COLS)
    diff = kc_idx - c_idx + NA_COLS - 1
    low_half = lax.broadcasted_iota(jnp.int32, (GRID_W, V7X_LANES), 1) < GRID_W
    for j in range(2):
        head = 2 * pair + j
        toeplitz = {}
        for dr in range(NA_ROWS - kh, NA_ROWS - 1 + kh):
            t = jnp.full((GRID_W, V7X_LANES), MASK_VALUE, F32)
            for dc in range(n_dc):
                t = jnp.where(diff == dc, rpb_ref[(head * n_dr + dr) * n_dc + dc] * LOG2E, t)
            toeplitz[dr] = jnp.where(valid, t, MASK_VALUE)
        for d in range(kh):
            for i in range(0, kh, 2):
                blk = jnp.where(low_half, toeplitz[i - d + NA_ROWS - 1], toeplitz[i + 1 - d + NA_ROWS - 1])
                tab_ref[d, j * GRID_W:(j + 1) * GRID_W, i * GRID_W:(i + 2) * GRID_W] = blk


def _attn_kernel(rpb_ref, q_ref, k_ref, v_ref, sg_ref, g_ref, o_ref, tab_ref):
    rows = q_ref.shape[1] // GRID_W
    kh = min(NA_ROWS, rows)
    nkeys = kh * GRID_W
    lane = lax.broadcasted_iota(jnp.int32, (GRID_W, V7X_LANES), 1)
    first = lane < NA_HEAD_DIM
    m0 = first.astype(F32).astype(BF16)
    m1 = (1.0 - first.astype(F32)).astype(BF16)
    gain = g_ref[...]

    @pl.when(pl.program_id(1) == 0)
    def _():
        _attn_build_table(rpb_ref, tab_ref, pl.program_id(0))

    def scores(r):
        rs = jnp.clip(r - kh // 2, 0, rows - kh)
        q0 = pl.multiple_of(r * GRID_W, GRID_W)
        k0 = pl.multiple_of(rs * GRID_W, GRID_W)
        q = q_ref[0, pl.ds(q0, GRID_W), :]
        kk = k_ref[0, pl.ds(k0, nkeys), :]
        qm = jnp.concatenate([m0 * q, m1 * q], axis=0)
        s = lax.dot_general(qm, kk, (((1,), (1,)), ((), ())), preferred_element_type=F32)
        return q0, k0, s + tab_ref[r - rs]

    def softmax(s):
        return jnp.exp2(s - jnp.max(s, axis=-1, keepdims=True)).astype(BF16)

    ones = jnp.ones((nkeys, V7X_LANES), BF16)

    def weighted_values(k0, p):
        vals = jnp.concatenate([v_ref[0, pl.ds(k0, nkeys), :], ones], axis=1)
        return jnp.dot(p, vals, preferred_element_type=F32)

    def finish(q0, o2):
        o2 = o2[:, :V7X_LANES] / o2[:, V7X_LANES:]
        o = jnp.where(first, o2[:GRID_W], o2[GRID_W:])
        sq = o * o
        ms0 = jnp.sum(jnp.where(first, sq, 0.0), axis=-1, keepdims=True)
        ms1 = jnp.sum(jnp.where(first, 0.0, sq), axis=-1, keepdims=True)
        ms = jnp.where(first, ms0, ms1) * (1.0 / NA_HEAD_DIM)
        y = o * lax.rsqrt(ms + NORM_EPS) * gain
        y = y * sg_ref[0, pl.ds(q0, GRID_W), :].astype(F32)
        o_ref[0, pl.ds(q0, GRID_W), :] = y.astype(BF16)

    per_iter = min(NA_ROWS_PER_ITER, rows)

    def body(i, carry):
        rs_ = [scores(i * per_iter + u) for u in range(per_iter)]
        ps = [softmax(s) for _, _, s in rs_]
        os_ = [weighted_values(k0, p) for (_, k0, _), p in zip(rs_, ps)]
        for (q0, _, _), o2 in zip(rs_, os_):
            finish(q0, o2)
        return carry

    lax.fori_loop(0, rows // per_iter, body, 0)


def _attn(rpb_flat, aq, ak, av, ag, attn_norm):
    b, s, _ = aq.shape
    kh = min(NA_ROWS, s // GRID_W)
    assert kh % 2 == 0
    blk = pl.BlockSpec((1, s, V7X_LANES), lambda hp, i: (i, 0, hp))
    return pl.pallas_call(
        _attn_kernel,
        grid=(NA_PAIRS, b),
        in_specs=[pl.BlockSpec(memory_space=pltpu.SMEM), blk, blk, blk, blk,
                  pl.BlockSpec((1, V7X_LANES), lambda hp, i: (0, hp))],
        out_specs=blk,
        out_shape=jax.ShapeDtypeStruct((b, s, NA_WIDTH), BF16),
        scratch_shapes=[pltpu.VMEM((kh, 2 * GRID_W, kh * GRID_W), F32)],
        compiler_params=pltpu.CompilerParams(
            dimension_semantics=("parallel", "arbitrary"), vmem_limit_bytes=V7X_VMEM_LIMIT),
        name="attn",
    )(rpb_flat, aq, ak, av, ag, attn_norm)


def _block_ref(b, n, rev):
    c, w = b.shape
    pos = n if rev else n - 1
    if 2 * n >= 8:
        b3 = b.reshape(c // (2 * n), 2 * n, w)
        return jnp.broadcast_to(b3[:, pos:pos + 1, :], b3.shape).reshape(c, w)
    row = lax.broadcasted_iota(jnp.int32, b.shape, 0)
    if n == 1:
        if rev:
            return jnp.where(row % 2 == 0, pltpu.roll(b, c - 1, 0), b)
        return jnp.where(row % 2 == 1, pltpu.roll(b, 1, 0), b)
    assert n == 2
    if rev:
        c1 = jnp.where(row % 2 == 1, pltpu.roll(b, 1, 0), b)
        return jnp.where(row % 4 < 2, pltpu.roll(c1, c - 2, 0), c1)
    c1 = jnp.where(row % 2 == 0, pltpu.roll(b, c - 1, 0), b)
    return jnp.where(row % 4 >= 2, pltpu.roll(c1, 2, 0), c1)


def _level_decay(b, n, rev):
    c, w = b.shape
    if n < 8:
        return jnp.exp2(-jnp.abs(b - _block_ref(b, n, rev)))
    b3 = b.reshape(c // (2 * n), 2 * n, w)
    lo, hi = b3[:, :n], b3[:, n:]
    if rev:
        ref = b3[:, n:n + 1]
        parts = [jnp.exp2(lo - ref), jnp.exp2(ref - hi)]
    else:
        ref = b3[:, n - 1:n]
        parts = [jnp.exp2(ref - lo), jnp.exp2(hi - ref)]
    return jnp.concatenate(parts, axis=1).reshape(c, w)


def _base_ref(b3, rev):
    base = b3.shape[1]
    pos = base // 2 if rev else base // 2 - 1
    return b3[:, pos:pos + 1]


def _level_ids(c, rev, base):
    t_idx = lax.broadcasted_iota(jnp.int32, (c, c), 0)
    s_idx = lax.broadcasted_iota(jnp.int32, (c, c), 1)
    x = t_idx ^ s_idx
    lev = jnp.ones((c, c), jnp.int32)
    n = 2 * base
    while n < c:
        lev = lev + (x >= n).astype(jnp.int32)
        n *= 2
    before = (t_idx < s_idx) if rev else (t_idx > s_idx)
    lev = jnp.where(before & (x >= base), lev, -1)
    return jnp.where((x < base) & (before | (t_idx == s_idx)), 0, lev)


def _two_head_dot(qs, ks):
    w = HG_HEAD_DIM
    zero = jnp.zeros_like(ks[:, :w])
    rhs = jnp.concatenate([jnp.concatenate([ks[:, :w], zero], axis=1),
                           jnp.concatenate([zero, ks[:, w:]], axis=1)], axis=0)
    return lax.dot_general(qs, rhs, (((1,), (1,)), ((), ())), preferred_element_type=F32)


def _hgrn_scores(q, kk, b, lev, rev, base):
    c, w = q.shape

    if base == 1:
        scores, growth, decay = _two_head_dot(q, kk), None, None
    else:
        b3 = b.reshape(c // base, base, w)
        d = (b3 - _base_ref(b3, rev)).reshape(c, w)
        growth, decay = jnp.exp2(-d).astype(BF16), jnp.exp2(d).astype(BF16)
        scores = _two_head_dot(decay * q, growth * kk)
    scores = jnp.where(lev == 0, scores, 0.0)
    n, li = base, 1
    while n < c:
        e = _level_decay(b, n, rev).astype(BF16)
        scores = jnp.where(lev == li, _two_head_dot(e * q, e * kk), scores)
        n, li = 2 * n, li + 1
    return scores.astype(BF16), growth, decay


def _hgrn_output(scores, q, kk, v, b, st, rev, growth, decay):
    c, w = q.shape
    btot = b[0:1, :] if rev else b[c - 1:c, :]
    if decay is None:
        from_start, to_end = jnp.exp2(b).astype(BF16), jnp.exp2(btot - b).astype(BF16)
    else:
        bmid = _base_ref(b.reshape(1, c, w), rev).reshape(1, w)
        from_start = decay * jnp.exp2(bmid).astype(BF16)
        to_end = growth * jnp.exp2(btot - bmid).astype(BF16)
    lhs = jnp.concatenate([scores, from_start * q], axis=1)
    rhs = jnp.concatenate([v, st.astype(BF16)], axis=0)
    o = jnp.dot(lhs, rhs, preferred_element_type=F32)

    kd = to_end * kk
    upd = lax.dot_general(kd, v, (((0,), (0,)), ((), ())), preferred_element_type=F32)
    keep = jnp.broadcast_to(jnp.exp2(btot), (w, w)).T
    return o, st * keep + upd


def _hgrn_kernel(qf_ref, vf_ref, lff_ref, qb_ref, vb_ref, lfb_ref,
                 of_ref, ob_ref, st_ref, b_ref):
    @pl.when(pl.program_id(1) == 0)
    def _():
        st_ref[...] = jnp.zeros_like(st_ref)

    c = HG_CHUNK
    n_sub = qf_ref.shape[1] // c
    t_idx = lax.broadcasted_iota(jnp.int32, (c, c), 0)
    u_idx = lax.broadcasted_iota(jnp.int32, (c, c), 1)
    dirs = ((0, False, qf_ref, vf_ref, lff_ref, of_ref), (1, True, qb_ref, vb_ref, lfb_ref, ob_ref))

    widest = jnp.zeros((c // HG_BASE, 1, lff_ref.shape[2]), F32)
    for di, rev, _, _, lf_ref, _ in dirs:
        tri = ((u_idx >= t_idx) if rev else (u_idx <= t_idx)).astype(F32).astype(BF16)
        for j in range(n_sub):
            rows = slice(j * c, (j + 1) * c)
            bj = jnp.dot(tri, lf_ref[0, rows, :], preferred_element_type=F32) * LOG2E
            b_ref[di, rows, :] = bj
            b3 = bj.reshape(c // HG_BASE, HG_BASE, bj.shape[1])
            ref = _base_ref(b3, rev)
            widest = jnp.maximum(widest, jnp.maximum(jnp.abs(b3[:, 0:1] - ref),
                                                     jnp.abs(b3[:, HG_BASE - 1:HG_BASE] - ref)))
    fast = jnp.max(widest) <= HG_MAX_LOG2_RANGE

    def run(base):
        lev = {rev: jnp.tile(_level_ids(c, rev, base), (1, 2)) for rev in (False, True)}
        hd = HG_HEAD_DIM

        def body(j, carry):
            work = []
            for di, rev, q_ref, v_ref, lf_ref, o_ref in dirs:
                r0 = ((n_sub - 1 - j) if rev else j) * c
                if not isinstance(j, int):
                    r0 = pl.multiple_of(r0, c)
                for h in range(0, HG_HEADS, 2):
                    cs2 = slice(h * hd, (h + 2) * hd)
                    ld = lambda ref: ref[0, pl.ds(r0, c), cs2]
                    q2, v2, bb2 = ld(q_ref), ld(v_ref), b_ref[di, pl.ds(r0, c), cs2]
                    kk2 = (1.0 - jnp.exp(ld(lf_ref).astype(F32))).astype(BF16)
                    scores2, growth2, decay2 = _hgrn_scores(q2, kk2, bb2, lev[rev], rev, base)
                    for u in range(2):
                        ls = slice(u * hd, (u + 1) * hd)
                        cut = lambda t: None if t is None else t[:, ls]
                        work.append((di, rev, h + u, r0, q2[:, ls], kk2[:, ls], v2[:, ls], bb2[:, ls],
                                     scores2[:, u * c:(u + 1) * c], cut(growth2), cut(decay2), o_ref))
            for di, rev, h, r0, q, kk, v, bb, scores, growth, decay, o_ref in work:
                o, st = _hgrn_output(scores, q, kk, v, bb, st_ref[di, h], rev, growth, decay)
                st_ref[di, h] = st
                o_ref[0, pl.ds(r0, c), h * hd:(h + 1) * hd] = o.astype(o_ref.dtype)
            return carry

        if base == 1:
            lax.fori_loop(0, n_sub, body, 0)
        else:
            for j in range(n_sub):
                body(j, 0)

    pl.when(fast)(lambda: run(HG_BASE))
    pl.when(jnp.logical_not(fast))(lambda: run(1))


def _hgrn(hq, hv, lff, lfb):
    b, s, w = hq.shape
    ns = s // HG_STEP
    fwd = pl.BlockSpec((1, HG_STEP, w), lambda i, j: (i, j, 0))
    bwd = pl.BlockSpec((1, HG_STEP, w), lambda i, j: (i, ns - 1 - j, 0))
    sds = jax.ShapeDtypeStruct((b, s, w), BF16)
    return pl.pallas_call(
        _hgrn_kernel,
        grid=(b, ns),
        in_specs=[fwd, fwd, fwd, bwd, bwd, bwd],
        out_specs=[fwd, bwd],
        out_shape=[sds, sds],
        scratch_shapes=[pltpu.VMEM((2, HG_HEADS, HG_HEAD_DIM, HG_HEAD_DIM), F32),
                        pltpu.VMEM((2, HG_STEP, w), F32)],
        compiler_params=pltpu.CompilerParams(
            dimension_semantics=("parallel", "arbitrary"), vmem_limit_bytes=V7X_VMEM_LIMIT),
        name="hgrn",
    )(hq, hv, lff, hq, hv, lfb)


def _rms(x, g):
    ms = jnp.mean(x * x, axis=-1, keepdims=True)
    return x * lax.rsqrt(ms + NORM_EPS) * g


def _out_kernel(a_ref, of_ref, ob_ref, hg_ref, x_ref, p_ref,
                gh_ref, wo_ref, gp_ref, wg_ref, wp_ref, gf_ref, o_ref):
    sub = x_ref.shape[0] // OUT_SPLIT
    tiles = [slice(u * sub, (u + 1) * sub) for u in range(OUT_SPLIT)]
    gh = gh_ref[...]

    def mixed(rows):
        r = of_ref[rows, :].astype(F32) + ob_ref[rows, :].astype(F32)
        parts = [a_ref[rows, :]]
        for h in range(HG_HEADS):
            cs = slice(h * HG_HEAD_DIM, (h + 1) * HG_HEAD_DIM)
            y = _rms(r[:, cs], gh[:, cs]) * hg_ref[rows, cs].astype(F32)
            parts.append(y.astype(BF16))
        return jnp.concatenate(parts, axis=-1)

    cats = [mixed(rows) for rows in tiles]
    pps = [jnp.dot(p_ref[rows, :].astype(BF16), wp_ref[...], preferred_element_type=F32)
           for rows in tiles]
    h1s = [x_ref[rows, :] + jnp.dot(cat, wo_ref[...], preferred_element_type=F32)
           for rows, cat in zip(tiles, cats)]
    gates = [_sigmoid(jnp.dot(_rms(h1, gp_ref[...]).astype(BF16), wg_ref[...],
                              preferred_element_type=F32)) for h1 in h1s]
    for rows, h1, gate, pp in zip(tiles, h1s, gates, pps):
        o_ref[rows, :] = _rms(h1 + gate * pp, gf_ref[...])


def _out(a, o_f, o_b, hg, x2, p2, hgrn_norm, w_out, ln_ple, w_pg, w_pp, ln_final):
    n = x2.shape[0]
    row = lambda w: pl.BlockSpec((OUT_TM, w), lambda i: (i, 0))
    full = lambda r, c: pl.BlockSpec((r, c), lambda i: (0, 0))
    return pl.pallas_call(
        _out_kernel,
        grid=(n // OUT_TM,),
        in_specs=[row(NA_WIDTH), row(HG_WIDTH), row(HG_WIDTH), row(HG_WIDTH),
                  row(D_MODEL), row(PLE_DIM),
                  full(1, HG_WIDTH), full(D_MODEL, D_MODEL), full(1, D_MODEL),
                  full(D_MODEL, D_MODEL), full(PLE_DIM, D_MODEL), full(1, D_MODEL)],
        out_specs=row(D_MODEL),
        out_shape=jax.ShapeDtypeStruct((n, D_MODEL), F32),
        compiler_params=pltpu.CompilerParams(
            dimension_semantics=("parallel",), vmem_limit_bytes=V7X_VMEM_LIMIT),
        name="out",
    )(a, o_f, o_b, hg, x2, p2, hgrn_norm, w_out, ln_ple, w_pg, w_pp, ln_final)


def kernel(x, p, ln_mix, w_in, rpb, lb_logits, attn_norm, hgrn_norm, w_out, ln_ple, w_pg, w_pp, ln_final):
    depth = p.shape[0]
    assert depth == 1 and lb_logits.shape[0] == 2
    b, s, d = x.shape
    n = b * s
    rows = s // GRID_W
    assert d == D_MODEL and n % PROJ_TM == 0 and n % OUT_TM == 0 and s % HG_STEP == 0
    assert rows % min(NA_ROWS_PER_ITER, rows) == 0 and rows >= NA_ROWS

    x2 = x.reshape(n, d)
    (aq, ak, av, ag, hq, lff, lfb, hv, hg) = _proj(
        x2, ln_mix[0:1], w_in[0].astype(BF16), lb_logits)

    r3 = lambda t: t.reshape(b, s, t.shape[-1])
    a = _attn(rpb[0].reshape(-1), r3(aq), r3(ak), r3(av), r3(ag), attn_norm[0:1])
    o_f, o_b = _hgrn(r3(hq), r3(hv), r3(lff), r3(lfb))

    out = _out(a.reshape(n, NA_WIDTH), o_f.reshape(n, HG_WIDTH), o_b.reshape(n, HG_WIDTH), hg,
               x2, p[0].reshape(n, PLE_DIM), hgrn_norm[0:1], w_out[0].astype(BF16),
               ln_ple[0:1], w_pg[0].astype(BF16), w_pp[0].astype(BF16), ln_final.reshape(1, d))
    return out.reshape(b, s, d)
```

```python
import jax
import jax.numpy as jnp
from jax import lax
from jax.experimental import pallas as pl
from jax.experimental.pallas import tpu as pltpu

F32 = jnp.float32
BF16 = jnp.bfloat16

D_MODEL = 1024
GRID_W = 64
NA_WIDTH = 512
HG_WIDTH = 512
NA_HEADS = 8
NA_HEAD_DIM = 64
NA_ROWS = 8
NA_COLS = 16
HG_HEAD_DIM = 128
HG_HEADS = 4
PLE_DIM = 256
NORM_EPS = 1e-6
N_GROUPS = 9
GROUP_WIDTH = NA_WIDTH
assert HG_WIDTH == GROUP_WIDTH
PROJ_TOTAL = N_GROUPS * GROUP_WIDTH

V7X_LANES = 128
V7X_VMEM_LIMIT = 56 * 1024 * 1024
MASK_VALUE = -1e30
LOG2E = 1.4426950408889634

PROJ_TM = 1024
PROJ_SPLIT = 8
OUT_TM = 1024
OUT_SPLIT = 2
HG_CHUNK = 128
HG_STEP = 1024
HG_BASE = HG_CHUNK
HG_MAX_LOG2_RANGE = 110.0
NA_PAIRS = NA_HEADS // 2
NA_ROWS_PER_ITER = 64


def _sigmoid(x):
    return 1.0 / (1.0 + jnp.exp(-x))


def _proj_kernel(x_ref, g_ref, w_ref, lbl_ref,
                 aq_ref, ak_ref, av_ref, ag_ref, hq_ref,
                 lff_ref, lfb_ref, hv_ref, hg_ref):
    l0 = lbl_ref[0]
    l1 = lbl_ref[1]
    m = jnp.maximum(l0, l1)
    e0 = jnp.exp(l0 - m)
    e1 = jnp.exp(l1 - m)
    lb = e0 / (e0 + e1)

    sub = x_ref.shape[0] // PROJ_SPLIT
    tiles = [slice(u * sub, (u + 1) * sub) for u in range(PROJ_SPLIT)]
    hns = []
    for rows in tiles:
        x = x_ref[rows, :]
        ms = jnp.mean(x * x, axis=-1, keepdims=True)
        hns.append((x * lax.rsqrt(ms + NORM_EPS) * g_ref[...]).astype(BF16))

    def silu(t):
        return t * _sigmoid(t)

    def forget(lbj):
        return lambda t: jnp.log(lbj + (1.0 - lbj) * _sigmoid(t))

    scale_q = lambda t: t * (NA_HEAD_DIM ** -0.5 * LOG2E)
    ident = lambda t: t
    groups = ((scale_q, (aq_ref,)), (ident, (ak_ref,)), (ident, (av_ref,)), (silu, (ag_ref,)),
              (silu, (hq_ref,)), (forget(lb[0:1, :]), (lff_ref,)),
              (forget(lb[1:2, :]), (lfb_ref,)), (ident, (hv_ref,)), (silu, (hg_ref,)))
    for j, (act, out_refs) in enumerate(groups):
        w = w_ref[:, j * GROUP_WIDTH:(j + 1) * GROUP_WIDTH]
        for rows, hn in zip(tiles, hns):
            res = act(jnp.dot(hn, w, preferred_element_type=F32))
            for ref in out_refs:
                ref[rows, :] = res.astype(BF16)


def _proj(x2, ln_mix, w_in, lb_logits):
    n = x2.shape[0]
    out_sds = jax.ShapeDtypeStruct((n, GROUP_WIDTH), BF16)
    row_spec = pl.BlockSpec((PROJ_TM, GROUP_WIDTH), lambda i: (i, 0))
    return pl.pallas_call(
        _proj_kernel,
        grid=(n // PROJ_TM,),
        in_specs=[
            pl.BlockSpec((PROJ_TM, D_MODEL), lambda i: (i, 0)),
            pl.BlockSpec((1, D_MODEL), lambda i: (0, 0)),
            pl.BlockSpec((D_MODEL, PROJ_TOTAL), lambda i: (0, 0), pipeline_mode=pl.Buffered(1)),
            pl.BlockSpec((2, 2, HG_WIDTH), lambda i: (0, 0, 0)),
        ],
        out_specs=[row_spec] * 9,
        out_shape=[out_sds] * 9,
        compiler_params=pltpu.CompilerParams(
            dimension_semantics=("parallel",), vmem_limit_bytes=V7X_VMEM_LIMIT),
        name="proj",
    )(x2, ln_mix, w_in, lb_logits)


def _attn_build_table(rpb_ref, tab_ref, pair):
    kh = tab_ref.shape[0]
    n_dr, n_dc = 2 * NA_ROWS - 1, 2 * NA_COLS - 1
    c_idx = lax.broadcasted_iota(jnp.int32, (GRID_W, V7X_LANES), 0)
    kc_idx = lax.broadcasted_iota(jnp.int32, (GRID_W, V7X_LANES), 1) % GRID_W
    start = jnp.clip(c_idx - NA_COLS // 2, 0, GRID_W - NA_COLS)
    valid = (kc_idx >= start) & (kc_idx < start + NA_COLS)
    diff = kc_idx - c_idx + NA_COLS - 1
    low_half = lax.broadcasted_iota(jnp.int32, (GRID_W, V7X_LANES), 1) < GRID_W
    for j in range(2):
        head = 2 * pair + j
        toeplitz = {}
        for dr in range(NA_ROWS - kh, NA_ROWS - 1 + kh):
            t = jnp.full((GRID_W, V7X_LANES), MASK_VALUE, F32)
            for dc in range(n_dc):
                t = jnp.where(diff == dc, rpb_ref[(head * n_dr + dr) * n_dc + dc] * LOG2E, t)
            toeplitz[dr] = jnp.where(valid, t, MASK_VALUE)
        for d in range(kh):
            for i in range(0, kh, 2):
                blk = jnp.where(low_half, toeplitz[i - d + NA_ROWS - 1], toeplitz[i + 1 - d + NA_ROWS - 1])
                tab_ref[d, j * GRID_W:(j + 1) * GRID_W, i * GRID_W:(i + 2) * GRID_W] = blk


def _attn_kernel(rpb_ref, q_ref, k_ref, v_ref, sg_ref, g_ref, o_ref, tab_ref):
    rows = q_ref.shape[1] // GRID_W
    kh = min(NA_ROWS, rows)
    nkeys = kh * GRID_W
    lane = lax.broadcasted_iota(jnp.int32, (GRID_W, V7X_LANES), 1)
    first = lane < NA_HEAD_DIM
    m0 = first.astype(F32).astype(BF16)
    m1 = (1.0 - first.astype(F32)).astype(BF16)
    gain = g_ref[...]

    @pl.when(pl.program_id(1) == 0)
    def _():
        _attn_build_table(rpb_ref, tab_ref, pl.program_id(0))

    def scores(r):
        rs = jnp.clip(r - kh // 2, 0, rows - kh)
        q0 = pl.multiple_of(r * GRID_W, GRID_W)
        k0 = pl.multiple_of(rs * GRID_W, GRID_W)
        q = q_ref[0, pl.ds(q0, GRID_W), :]
        kk = k_ref[0, pl.ds(k0, nkeys), :]
        qm = jnp.concatenate([m0 * q, m1 * q], axis=0)
        s = lax.dot_general(qm, kk, (((1,), (1,)), ((), ())), preferred_element_type=F32)
        return q0, k0, s + tab_ref[r - rs]

    def softmax(s):
        return jnp.exp2(s - jnp.max(s, axis=-1, keepdims=True)).astype(BF16)

    ones = jnp.ones((nkeys, V7X_LANES), BF16)

    def weighted_values(k0, p):
        vals = jnp.concatenate([v_ref[0, pl.ds(k0, nkeys), :], ones], axis=1)
        return jnp.dot(p, vals, preferred_element_type=F32)

    def finish(q0, o2):
        o2 = o2[:, :V7X_LANES] / o2[:, V7X_LANES:]
        o = jnp.where(first, o2[:GRID_W], o2[GRID_W:])
        sq = o * o
        ms0 = jnp.sum(jnp.where(first, sq, 0.0), axis=-1, keepdims=True)
        ms1 = jnp.sum(jnp.where(first, 0.0, sq), axis=-1, keepdims=True)
        ms = jnp.where(first, ms0, ms1) * (1.0 / NA_HEAD_DIM)
        y = o * lax.rsqrt(ms + NORM_EPS) * gain
        y = y * sg_ref[0, pl.ds(q0, GRID_W), :].astype(F32)
        o_ref[0, pl.ds(q0, GRID_W), :] = y.astype(BF16)

    per_iter = min(NA_ROWS_PER_ITER, rows)

    def body(i, carry):
        rs_ = [scores(i * per_iter + u) for u in range(per_iter)]
        ps = [softmax(s) for _, _, s in rs_]
        os_ = [weighted_values(k0, p) for (_, k0, _), p in zip(rs_, ps)]
        for (q0, _, _), o2 in zip(rs_, os_):
            finish(q0, o2)
        return carry

    lax.fori_loop(0, rows // per_iter, body, 0)


def _attn(rpb_flat, aq, ak, av, ag, attn_norm):
    b, s, _ = aq.shape
    kh = min(NA_ROWS, s // GRID_W)
    assert kh % 2 == 0
    blk = pl.BlockSpec((1, s, V7X_LANES), lambda hp, i: (i, 0, hp))
    return pl.pallas_call(
        _attn_kernel,
        grid=(NA_PAIRS, b),
        in_specs=[pl.BlockSpec(memory_space=pltpu.SMEM), blk, blk, blk, blk,
                  pl.BlockSpec((1, V7X_LANES), lambda hp, i: (0, hp))],
        out_specs=blk,
        out_shape=jax.ShapeDtypeStruct((b, s, NA_WIDTH), BF16),
        scratch_shapes=[pltpu.VMEM((kh, 2 * GRID_W, kh * GRID_W), F32)],
        compiler_params=pltpu.CompilerParams(
            dimension_semantics=("parallel", "arbitrary"), vmem_limit_bytes=V7X_VMEM_LIMIT),
        name="attn",
    )(rpb_flat, aq, ak, av, ag, attn_norm)


def _block_ref(b, n, rev):
    c, w = b.shape
    pos = n if rev else n - 1
    if 2 * n >= 8:
        b3 = b.reshape(c // (2 * n), 2 * n, w)
        return jnp.broadcast_to(b3[:, pos:pos + 1, :], b3.shape).reshape(c, w)
    row = lax.broadcasted_iota(jnp.int32, b.shape, 0)
    if n == 1:
        if rev:
            return jnp.where(row % 2 == 0, pltpu.roll(b, c - 1, 0), b)
        return jnp.where(row % 2 == 1, pltpu.roll(b, 1, 0), b)
    assert n == 2
    if rev:
        c1 = jnp.where(row % 2 == 1, pltpu.roll(b, 1, 0), b)
        return jnp.where(row % 4 < 2, pltpu.roll(c1, c - 2, 0), c1)
    c1 = jnp.where(row % 2 == 0, pltpu.roll(b, c - 1, 0), b)
    return jnp.where(row % 4 >= 2, pltpu.roll(c1, 2, 0), c1)


def _level_decay(b, n, rev):
    c, w = b.shape
    if n < 8:
        return jnp.exp2(-jnp.abs(b - _block_ref(b, n, rev)))
    b3 = b.reshape(c // (2 * n), 2 * n, w)
    lo, hi = b3[:, :n], b3[:, n:]
    if rev:
        ref = b3[:, n:n + 1]
        parts = [jnp.exp2(lo - ref), jnp.exp2(ref - hi)]
    else:
        ref = b3[:, n - 1:n]
        parts = [jnp.exp2(ref - lo), jnp.exp2(hi - ref)]
    return jnp.concatenate(parts, axis=1).reshape(c, w)


def _base_ref(b3, rev):
    base = b3.shape[1]
    pos = base // 2 if rev else base // 2 - 1
    return b3[:, pos:pos + 1]


def _level_ids(c, rev, base):
    t_idx = lax.broadcasted_iota(jnp.int32, (c, c), 0)
    s_idx = lax.broadcasted_iota(jnp.int32, (c, c), 1)
    x = t_idx ^ s_idx
    lev = jnp.ones((c, c), jnp.int32)
    n = 2 * base
    while n < c:
        lev = lev + (x >= n).astype(jnp.int32)
        n *= 2
    before = (t_idx < s_idx) if rev else (t_idx > s_idx)
    lev = jnp.where(before & (x >= base), lev, -1)
    return jnp.where((x < base) & (before | (t_idx == s_idx)), 0, lev)


def _two_head_dot(qs, ks):
    w = HG_HEAD_DIM
    zero = jnp.zeros_like(ks[:, :w])
    rhs = jnp.concatenate([jnp.concatenate([ks[:, :w], zero], axis=1),
                           jnp.concatenate([zero, ks[:, w:]], axis=1)], axis=0)
    return lax.dot_general(qs, rhs, (((1,), (1,)), ((), ())), preferred_element_type=F32)


def _hgrn_scores(q, kk, b, lev, rev, base):
    c, w = q.shape

    if base == 1:
        scores, growth, decay = _two_head_dot(q, kk), None, None
    else:
        b3 = b.reshape(c // base, base, w)
        d = (b3 - _base_ref(b3, rev)).reshape(c, w)
        growth, decay = jnp.exp2(-d).astype(BF16), jnp.exp2(d).astype(BF16)
        scores = _two_head_dot(decay * q, growth * kk)
    scores = jnp.where(lev == 0, scores, 0.0)
    n, li = base, 1
    while n < c:
        e = _level_decay(b, n, rev).astype(BF16)
        scores = jnp.where(lev == li, _two_head_dot(e * q, e * kk), scores)
        n, li = 2 * n, li + 1
    return scores.astype(BF16), growth, decay


def _hgrn_output(scores, q, kk, v, b, st, rev, growth, decay):
    c, w = q.shape
    btot = b[0:1, :] if rev else b[c - 1:c, :]
    if decay is None:
        from_start, to_end = jnp.exp2(b).astype(BF16), jnp.exp2(btot - b).astype(BF16)
    else:
        bmid = _base_ref(b.reshape(1, c, w), rev).reshape(1, w)
        from_start = decay * jnp.exp2(bmid).astype(BF16)
        to_end = growth * jnp.exp2(btot - bmid).astype(BF16)
    lhs = jnp.concatenate([scores, from_start * q], axis=1)
    rhs = jnp.concatenate([v, st.astype(BF16)], axis=0)
    o = jnp.dot(lhs, rhs, preferred_element_type=F32)

    kd = to_end * kk
    upd = lax.dot_general(kd, v, (((0,), (0,)), ((), ())), preferred_element_type=F32)
    keep = jnp.broadcast_to(jnp.exp2(btot), (w, w)).T
    return o, st * keep + upd


def _hgrn_kernel(qf_ref, vf_ref, lff_ref, qb_ref, vb_ref, lfb_ref,
                 of_ref, ob_ref, st_ref, b_ref):
    @pl.when(pl.program_id(1) == 0)
    def _():
        st_ref[...] = jnp.zeros_like(st_ref)

    c = HG_CHUNK
    n_sub = qf_ref.shape[1] // c
    t_idx = lax.broadcasted_iota(jnp.int32, (c, c), 0)
    u_idx = lax.broadcasted_iota(jnp.int32, (c, c), 1)
    dirs = ((0, False, qf_ref, vf_ref, lff_ref, of_ref), (1, True, qb_ref, vb_ref, lfb_ref, ob_ref))

    widest = jnp.zeros((c // HG_BASE, 1, lff_ref.shape[2]), F32)
    for di, rev, _, _, lf_ref, _ in dirs:
        tri = ((u_idx >= t_idx) if rev else (u_idx <= t_idx)).astype(F32).astype(BF16)
        for j in range(n_sub):
            rows = slice(j * c, (j + 1) * c)
            bj = jnp.dot(tri, lf_ref[0, rows, :], preferred_element_type=F32) * LOG2E
            b_ref[di, rows, :] = bj
            b3 = bj.reshape(c // HG_BASE, HG_BASE, bj.shape[1])
            ref = _base_ref(b3, rev)
            widest = jnp.maximum(widest, jnp.maximum(jnp.abs(b3[:, 0:1] - ref),
                                                     jnp.abs(b3[:, HG_BASE - 1:HG_BASE] - ref)))
    fast = jnp.max(widest) <= HG_MAX_LOG2_RANGE

    def run(base):
        lev = {rev: jnp.tile(_level_ids(c, rev, base), (1, 2)) for rev in (False, True)}
        hd = HG_HEAD_DIM

        def body(j, carry):
            work = []
            for di, rev, q_ref, v_ref, lf_ref, o_ref in dirs:
                r0 = ((n_sub - 1 - j) if rev else j) * c
                if not isinstance(j, int):
                    r0 = pl.multiple_of(r0, c)
                for h in range(0, HG_HEADS, 2):
                    cs2 = slice(h * hd, (h + 2) * hd)
                    ld = lambda ref: ref[0, pl.ds(r0, c), cs2]
                    q2, v2, bb2 = ld(q_ref), ld(v_ref), b_ref[di, pl.ds(r0, c), cs2]
                    kk2 = (1.0 - jnp.exp(ld(lf_ref).astype(F32))).astype(BF16)
                    scores2, growth2, decay2 = _hgrn_scores(q2, kk2, bb2, lev[rev], rev, base)
                    for u in range(2):
                        ls = slice(u * hd, (u + 1) * hd)
                        cut = lambda t: None if t is None else t[:, ls]
                        work.append((di, rev, h + u, r0, q2[:, ls], kk2[:, ls], v2[:, ls], bb2[:, ls],
                                     scores2[:, u * c:(u + 1) * c], cut(growth2), cut(decay2), o_ref))
            for di, rev, h, r0, q, kk, v, bb, scores, growth, decay, o_ref in work:
                o, st = _hgrn_output(scores, q, kk, v, bb, st_ref[di, h], rev, growth, decay)
                st_ref[di, h] = st
                o_ref[0, pl.ds(r0, c), h * hd:(h + 1) * hd] = o.astype(o_ref.dtype)
            return carry

        if base == 1:
            lax.fori_loop(0, n_sub, body, 0)
        else:
            for j in range(n_sub):
                body(j, 0)

    pl.when(fast)(lambda: run(HG_BASE))
    pl.when(jnp.logical_not(fast))(lambda: run(1))


def _hgrn(hq, hv, lff, lfb):
    b, s, w = hq.shape
    ns = s // HG_STEP
    fwd = pl.BlockSpec((1, HG_STEP, w), lambda i, j: (i, j, 0))
    bwd = pl.BlockSpec((1, HG_STEP, w), lambda i, j: (i, ns - 1 - j, 0))
    sds = jax.ShapeDtypeStruct((b, s, w), BF16)
    return pl.pallas_call(
        _hgrn_kernel,
        grid=(b, ns),
        in_specs=[fwd, fwd, fwd, bwd, bwd, bwd],
        out_specs=[fwd, bwd],
        out_shape=[sds, sds],
        scratch_shapes=[pltpu.VMEM((2, HG_HEADS, HG_HEAD_DIM, HG_HEAD_DIM), F32),
                        pltpu.VMEM((2, HG_STEP, w), F32)],
        compiler_params=pltpu.CompilerParams(
            dimension_semantics=("parallel", "arbitrary"), vmem_limit_bytes=V7X_VMEM_LIMIT),
        name="hgrn",
    )(hq, hv, lff, hq, hv, lfb)


def _rms(x, g):
    ms = jnp.mean(x * x, axis=-1, keepdims=True)
    return x * lax.rsqrt(ms + NORM_EPS) * g


def _out_kernel(a_ref, of_ref, ob_ref, hg_ref, x_ref, p_ref,
                gh_ref, wo_ref, gp_ref, wg_ref, wp_ref, gf_ref, o_ref):
    sub = x_ref.shape[0] // OUT_SPLIT
    tiles = [slice(u * sub, (u + 1) * sub) for u in range(OUT_SPLIT)]
    gh = gh_ref[...]

    def mixed(rows):
        r = of_ref[rows, :].astype(F32) + ob_ref[rows, :].astype(F32)
        parts = [a_ref[rows, :]]
        for h in range(HG_HEADS):
            cs = slice(h * HG_HEAD_DIM, (h + 1) * HG_HEAD_DIM)
            y = _rms(r[:, cs], gh[:, cs]) * hg_ref[rows, cs].astype(F32)
            parts.append(y.astype(BF16))
        return jnp.concatenate(parts, axis=-1)

    cats = [mixed(rows) for rows in tiles]
    pps = [jnp.dot(p_ref[rows, :].astype(BF16), wp_ref[...], preferred_element_type=F32)
           for rows in tiles]
    h1s = [x_ref[rows, :] + jnp.dot(cat, wo_ref[...], preferred_element_type=F32)
           for rows, cat in zip(tiles, cats)]
    gates = [_sigmoid(jnp.dot(_rms(h1, gp_ref[...]).astype(BF16), wg_ref[...],
                              preferred_element_type=F32)) for h1 in h1s]
    for rows, h1, gate, pp in zip(tiles, h1s, gates, pps):
        o_ref[rows, :] = _rms(h1 + gate * pp, gf_ref[...])


def _out(a, o_f, o_b, hg, x2, p2, hgrn_norm, w_out, ln_ple, w_pg, w_pp, ln_final):
    n = x2.shape[0]
    row = lambda w: pl.BlockSpec((OUT_TM, w), lambda i: (i, 0))
    full = lambda r, c: pl.BlockSpec((r, c), lambda i: (0, 0))
    return pl.pallas_call(
        _out_kernel,
        grid=(n // OUT_TM,),
        in_specs=[row(NA_WIDTH), row(HG_WIDTH), row(HG_WIDTH), row(HG_WIDTH),
                  row(D_MODEL), row(PLE_DIM),
                  full(1, HG_WIDTH), full(D_MODEL, D_MODEL), full(1, D_MODEL),
                  full(D_MODEL, D_MODEL), full(PLE_DIM, D_MODEL), full(1, D_MODEL)],
        out_specs=row(D_MODEL),
        out_shape=jax.ShapeDtypeStruct((n, D_MODEL), F32),
        compiler_params=pltpu.CompilerParams(
            dimension_semantics=("parallel",), vmem_limit_bytes=V7X_VMEM_LIMIT),
        name="out",
    )(a, o_f, o_b, hg, x2, p2, hgrn_norm, w_out, ln_ple, w_pg, w_pp, ln_final)


def kernel(x, p, ln_mix, w_in, rpb, lb_logits, attn_norm, hgrn_norm, w_out, ln_ple, w_pg, w_pp, ln_final):
    depth = p.shape[0]
    assert depth == 1 and lb_logits.shape[0] == 2
    b, s, d = x.shape
    n = b * s
    rows = s // GRID_W
    assert d == D_MODEL and n % PROJ_TM == 0 and n % OUT_TM == 0 and s % HG_STEP == 0
    assert rows % min(NA_ROWS_PER_ITER, rows) == 0 and rows >= NA_ROWS

    x2 = x.reshape(n, d)
    (aq, ak, av, ag, hq, lff, lfb, hv, hg) = _proj(
        x2, ln_mix[0:1], w_in[0].astype(BF16), lb_logits)

    r3 = lambda t: t.reshape(b, s, t.shape[-1])
    a = _attn(rpb[0].reshape(-1), r3(aq), r3(ak), r3(av), r3(ag), attn_norm[0:1])
    o_f, o_b = _hgrn(r3(hq), r3(hv), r3(lff), r3(lfb))

    out = _out(a.reshape(n, NA_WIDTH), o_f.reshape(n, HG_WIDTH), o_b.reshape(n, HG_WIDTH), hg,
               x2, p[0].reshape(n, PLE_DIM), hgrn_norm[0:1], w_out[0].astype(BF16),
               ln_ple[0:1], w_pg[0].astype(BF16), w_pp[0].astype(BF16), ln_final.reshape(1, d))
    return out.reshape(b, s, d)
```

```python
import jax
import jax.numpy as jnp
from jax import lax
from jax.experimental import pallas as pl
from jax.experimental.pallas import tpu as pltpu

F32 = jnp.float32
BF16 = jnp.bfloat16

D_MODEL = 1024
GRID_W = 64
NA_WIDTH = 512
HG_WIDTH = 512
NA_HEADS = 8
NA_HEAD_DIM = 64
NA_ROWS = 8
NA_COLS = 16
HG_HEAD_DIM = 128
HG_HEADS = 4
PLE_DIM = 256
NORM_EPS = 1e-6
N_GROUPS = 9
GROUP_WIDTH = NA_WIDTH
assert HG_WIDTH == GROUP_WIDTH
PROJ_TOTAL = N_GROUPS * GROUP_WIDTH

V7X_LANES = 128
V7X_VMEM_LIMIT = 56 * 1024 * 1024
MASK_VALUE = -1e30
LOG2E = 1.4426950408889634

PROJ_TM = 1024
PROJ_SPLIT = 8
OUT_TM = 1024
OUT_SPLIT = 2
HG_CHUNK = 128
HG_STEP = 2048
HG_BASE = HG_CHUNK
HG_MAX_LOG2_RANGE = 110.0
NA_PAIRS = NA_HEADS // 2
NA_ROWS_PER_ITER = 64


def _sigmoid(x):
    return 1.0 / (1.0 + jnp.exp(-x))


def _proj_kernel(x_ref, g_ref, w_ref, lbl_ref,
                 aq_ref, ak_ref, av_ref, ag_ref, hq_ref,
                 lff_ref, kf_ref, lfb_ref, kb_ref, hv_ref, hg_ref):
    l0 = lbl_ref[0]
    l1 = lbl_ref[1]
    m = jnp.maximum(l0, l1)
    e0 = jnp.exp(l0 - m)
    e1 = jnp.exp(l1 - m)
    lb = e0 / (e0 + e1)

    sub = x_ref.shape[0] // PROJ_SPLIT
    tiles = [slice(u * sub, (u + 1) * sub) for u in range(PROJ_SPLIT)]
    hns = []
    for rows in tiles:
        x = x_ref[rows, :]
        ms = jnp.mean(x * x, axis=-1, keepdims=True)
        hns.append((x * lax.rsqrt(ms + NORM_EPS) * g_ref[...]).astype(BF16))

    def silu(t):
        return t * _sigmoid(t)

    def forget(lbj):
        def act(t):
            f = lbj + (1.0 - lbj) * _sigmoid(t)
            return jnp.log(f), 1.0 - f
        return act

    scale_q = lambda t: t * (NA_HEAD_DIM ** -0.5 * LOG2E)
    ident = lambda t: t
    groups = ((scale_q, (aq_ref,)), (ident, (ak_ref,)), (ident, (av_ref,)), (silu, (ag_ref,)),
              (silu, (hq_ref,)), (forget(lb[0:1, :]), (lff_ref, kf_ref)),
              (forget(lb[1:2, :]), (lfb_ref, kb_ref)), (ident, (hv_ref,)), (silu, (hg_ref,)))
    for j, (act, out_refs) in enumerate(groups):
        w = w_ref[:, j * GROUP_WIDTH:(j + 1) * GROUP_WIDTH]
        for rows, hn in zip(tiles, hns):
            res = act(jnp.dot(hn, w, preferred_element_type=F32))
            for ref, val in zip(out_refs, res if isinstance(res, tuple) else (res,)):
                ref[rows, :] = val.astype(BF16)


def _proj(x2, ln_mix, w_in, lb_logits):
    n = x2.shape[0]
    out_sds = jax.ShapeDtypeStruct((n, GROUP_WIDTH), BF16)
    row_spec = pl.BlockSpec((PROJ_TM, GROUP_WIDTH), lambda i: (i, 0))
    return pl.pallas_call(
        _proj_kernel,
        grid=(n // PROJ_TM,),
        in_specs=[
            pl.BlockSpec((PROJ_TM, D_MODEL), lambda i: (i, 0)),
            pl.BlockSpec((1, D_MODEL), lambda i: (0, 0)),
            pl.BlockSpec((D_MODEL, PROJ_TOTAL), lambda i: (0, 0), pipeline_mode=pl.Buffered(1)),
            pl.BlockSpec((2, 2, HG_WIDTH), lambda i: (0, 0, 0)),
        ],
        out_specs=[row_spec] * 11,
        out_shape=[out_sds] * 11,
        compiler_params=pltpu.CompilerParams(
            dimension_semantics=("parallel",), vmem_limit_bytes=V7X_VMEM_LIMIT),
        name="proj",
    )(x2, ln_mix, w_in, lb_logits)


def _attn_build_table(rpb_ref, tab_ref, pair):
    kh = tab_ref.shape[0]
    n_dr, n_dc = 2 * NA_ROWS - 1, 2 * NA_COLS - 1
    c_idx = lax.broadcasted_iota(jnp.int32, (GRID_W, V7X_LANES), 0)
    kc_idx = lax.broadcasted_iota(jnp.int32, (GRID_W, V7X_LANES), 1) % GRID_W
    start = jnp.clip(c_idx - NA_COLS // 2, 0, GRID_W - NA_COLS)
    valid = (kc_idx >= start) & (kc_idx < start + NA_COLS)
    diff = kc_idx - c_idx + NA_COLS - 1
    low_half = lax.broadcasted_iota(jnp.int32, (GRID_W, V7X_LANES), 1) < GRID_W
    for j in range(2):
        head = 2 * pair + j
        toeplitz = {}
        for dr in range(NA_ROWS - kh, NA_ROWS - 1 + kh):
            t = jnp.full((GRID_W, V7X_LANES), MASK_VALUE, F32)
            for dc in range(n_dc):
                t = jnp.where(diff == dc, rpb_ref[(head * n_dr + dr) * n_dc + dc] * LOG2E, t)
            toeplitz[dr] = jnp.where(valid, t, MASK_VALUE)
        for d in range(kh):
            for i in range(0, kh, 2):
                blk = jnp.where(low_half, toeplitz[i - d + NA_ROWS - 1], toeplitz[i + 1 - d + NA_ROWS - 1])
                tab_ref[d, j * GRID_W:(j + 1) * GRID_W, i * GRID_W:(i + 2) * GRID_W] = blk


def _attn_kernel(rpb_ref, q_ref, k_ref, v_ref, sg_ref, g_ref, o_ref, tab_ref):
    rows = q_ref.shape[1] // GRID_W
    kh = min(NA_ROWS, rows)
    nkeys = kh * GRID_W
    lane = lax.broadcasted_iota(jnp.int32, (GRID_W, V7X_LANES), 1)
    first = lane < NA_HEAD_DIM
    m0 = first.astype(F32).astype(BF16)
    m1 = (1.0 - first.astype(F32)).astype(BF16)
    gain = g_ref[...]

    @pl.when(pl.program_id(1) == 0)
    def _():
        _attn_build_table(rpb_ref, tab_ref, pl.program_id(0))

    def scores(r):
        rs = jnp.clip(r - kh // 2, 0, rows - kh)
        q0 = pl.multiple_of(r * GRID_W, GRID_W)
        k0 = pl.multiple_of(rs * GRID_W, GRID_W)
        q = q_ref[0, pl.ds(q0, GRID_W), :]
        kk = k_ref[0, pl.ds(k0, nkeys), :]
        qm = jnp.concatenate([m0 * q, m1 * q], axis=0)
        s = lax.dot_general(qm, kk, (((1,), (1,)), ((), ())), preferred_element_type=F32)
        return q0, k0, s + tab_ref[r - rs]

    def softmax(s):
        return jnp.exp2(s - jnp.max(s, axis=-1, keepdims=True)).astype(BF16)

    ones = jnp.ones((nkeys, V7X_LANES), BF16)

    def weighted_values(k0, p):
        vals = jnp.concatenate([v_ref[0, pl.ds(k0, nkeys), :], ones], axis=1)
        return jnp.dot(p, vals, preferred_element_type=F32)

    def finish(q0, o2):
        o2 = o2[:, :V7X_LANES] / o2[:, V7X_LANES:]
        o = jnp.where(first, o2[:GRID_W], o2[GRID_W:])
        sq = o * o
        ms0 = jnp.sum(jnp.where(first, sq, 0.0), axis=-1, keepdims=True)
        ms1 = jnp.sum(jnp.where(first, 0.0, sq), axis=-1, keepdims=True)
        ms = jnp.where(first, ms0, ms1) * (1.0 / NA_HEAD_DIM)
        y = o * lax.rsqrt(ms + NORM_EPS) * gain
        y = y * sg_ref[0, pl.ds(q0, GRID_W), :].astype(F32)
        o_ref[0, pl.ds(q0, GRID_W), :] = y.astype(BF16)

    per_iter = min(NA_ROWS_PER_ITER, rows)

    def body(i, carry):
        rs_ = [scores(i * per_iter + u) for u in range(per_iter)]
        ps = [softmax(s) for _, _, s in rs_]
        os_ = [weighted_values(k0, p) for (_, k0, _), p in zip(rs_, ps)]
        for (q0, _, _), o2 in zip(rs_, os_):
            finish(q0, o2)
        return carry

    lax.fori_loop(0, rows // per_iter, body, 0)


def _attn(rpb_flat, aq, ak, av, ag, attn_norm):
    b, s, _ = aq.shape
    kh = min(NA_ROWS, s // GRID_W)
    assert kh % 2 == 0
    blk = pl.BlockSpec((1, s, V7X_LANES), lambda hp, i: (i, 0, hp))
    return pl.pallas_call(
        _attn_kernel,
        grid=(NA_PAIRS, b),
        in_specs=[pl.BlockSpec(memory_space=pltpu.SMEM), blk, blk, blk, blk,
                  pl.BlockSpec((1, V7X_LANES), lambda hp, i: (0, hp))],
        out_specs=blk,
        out_shape=jax.ShapeDtypeStruct((b, s, NA_WIDTH), BF16),
        scratch_shapes=[pltpu.VMEM((kh, 2 * GRID_W, kh * GRID_W), F32)],
        compiler_params=pltpu.CompilerParams(
            dimension_semantics=("parallel", "arbitrary"), vmem_limit_bytes=V7X_VMEM_LIMIT),
        name="attn",
    )(rpb_flat, aq, ak, av, ag, attn_norm)


def _block_ref(b, n, rev):
    c, w = b.shape
    pos = n if rev else n - 1
    if 2 * n >= 8:
        b3 = b.reshape(c // (2 * n), 2 * n, w)
        return jnp.broadcast_to(b3[:, pos:pos + 1, :], b3.shape).reshape(c, w)
    row = lax.broadcasted_iota(jnp.int32, b.shape, 0)
    if n == 1:
        if rev:
            return jnp.where(row % 2 == 0, pltpu.roll(b, c - 1, 0), b)
        return jnp.where(row % 2 == 1, pltpu.roll(b, 1, 0), b)
    assert n == 2
    if rev:
        c1 = jnp.where(row % 2 == 1, pltpu.roll(b, 1, 0), b)
        return jnp.where(row % 4 < 2, pltpu.roll(c1, c - 2, 0), c1)
    c1 = jnp.where(row % 2 == 0, pltpu.roll(b, c - 1, 0), b)
    return jnp.where(row % 4 >= 2, pltpu.roll(c1, 2, 0), c1)


def _level_decay(b, n, rev):
    c, w = b.shape
    if n < 8:
        return jnp.exp2(-jnp.abs(b - _block_ref(b, n, rev)))
    b3 = b.reshape(c // (2 * n), 2 * n, w)
    lo, hi = b3[:, :n], b3[:, n:]
    if rev:
        ref = b3[:, n:n + 1]
        parts = [jnp.exp2(lo - ref), jnp.exp2(ref - hi)]
    else:
        ref = b3[:, n - 1:n]
        parts = [jnp.exp2(ref - lo), jnp.exp2(hi - ref)]
    return jnp.concatenate(parts, axis=1).reshape(c, w)


def _base_ref(b3, rev):
    base = b3.shape[1]
    pos = base // 2 if rev else base // 2 - 1
    return b3[:, pos:pos + 1]


def _level_ids(c, rev, base):
    t_idx = lax.broadcasted_iota(jnp.int32, (c, c), 0)
    s_idx = lax.broadcasted_iota(jnp.int32, (c, c), 1)
    x = t_idx ^ s_idx
    lev = jnp.ones((c, c), jnp.int32)
    n = 2 * base
    while n < c:
        lev = lev + (x >= n).astype(jnp.int32)
        n *= 2
    before = (t_idx < s_idx) if rev else (t_idx > s_idx)
    lev = jnp.where(before & (x >= base), lev, -1)
    return jnp.where((x < base) & (before | (t_idx == s_idx)), 0, lev)


def _hgrn_scores(q, kk, b, lev, rev, base):
    c, w = q.shape
    nt = (((1,), (1,)), ((), ()))

    if base == 1:
        scores = lax.dot_general(q, kk, nt, preferred_element_type=F32)
    else:
        b3 = b.reshape(c // base, base, w)
        d = (b3 - _base_ref(b3, rev)).reshape(c, w)
        scores = lax.dot_general(jnp.exp2(d).astype(BF16) * q, jnp.exp2(-d).astype(BF16) * kk, nt,
                                 preferred_element_type=F32)
    scores = jnp.where(lev == 0, scores, 0.0)
    n, li = base, 1
    while n < c:
        e = _level_decay(b, n, rev).astype(BF16)
        sl = lax.dot_general(e * q, e * kk, nt, preferred_element_type=F32)
        scores = jnp.where(lev == li, sl, scores)
        n, li = 2 * n, li + 1
    return scores.astype(BF16)


def _hgrn_output(scores, q, kk, v, b, st, rev):
    c, w = q.shape
    btot = b[0:1, :] if rev else b[c - 1:c, :]
    lhs = jnp.concatenate([scores, jnp.exp2(b).astype(BF16) * q], axis=1)
    rhs = jnp.concatenate([v, st.astype(BF16)], axis=0)
    o = jnp.dot(lhs, rhs, preferred_element_type=F32)

    kd = jnp.exp2(btot - b).astype(BF16) * kk
    upd = lax.dot_general(kd, v, (((0,), (0,)), ((), ())), preferred_element_type=F32)
    keep = jnp.broadcast_to(jnp.exp2(btot), (w, w)).T
    return o, st * keep + upd


def _hgrn_kernel(qf_ref, vf_ref, lff_ref, kf_ref, qb_ref, vb_ref, lfb_ref, kb_ref,
                 of_ref, ob_ref, st_ref, b_ref):
    @pl.when(pl.program_id(1) == 0)
    def _():
        st_ref[...] = jnp.zeros_like(st_ref)

    c = HG_CHUNK
    n_sub = qf_ref.shape[1] // c
    t_idx = lax.broadcasted_iota(jnp.int32, (c, c), 0)
    u_idx = lax.broadcasted_iota(jnp.int32, (c, c), 1)
    dirs = ((0, False, qf_ref, kf_ref, vf_ref, lff_ref, of_ref),
            (1, True, qb_ref, kb_ref, vb_ref, lfb_ref, ob_ref))

    widest = jnp.zeros((c // HG_BASE, 1, lff_ref.shape[2]), F32)
    for di, rev, _, _, _, lf_ref, _ in dirs:
        tri = ((u_idx >= t_idx) if rev else (u_idx <= t_idx)).astype(F32).astype(BF16)
        for j in range(n_sub):
            rows = slice(j * c, (j + 1) * c)
            bj = jnp.dot(tri, lf_ref[0, rows, :], preferred_element_type=F32) * LOG2E
            b_ref[di, rows, :] = bj
            b3 = bj.reshape(c // HG_BASE, HG_BASE, bj.shape[1])
            ref = _base_ref(b3, rev)
            widest = jnp.maximum(widest, jnp.maximum(jnp.abs(b3[:, 0:1] - ref),
                                                     jnp.abs(b3[:, HG_BASE - 1:HG_BASE] - ref)))
    fast = jnp.max(widest) <= HG_MAX_LOG2_RANGE

    def run(base):
        lev = {rev: _level_ids(c, rev, base) for rev in (False, True)}

        def body(j, carry):
            work = []
            for di, rev, q_ref, k_ref, v_ref, _, o_ref in dirs:
                r0 = ((n_sub - 1 - j) if rev else j) * c
                if not isinstance(j, int):
                    r0 = pl.multiple_of(r0, c)
                for h in range(HG_HEADS):
                    cs = slice(h * HG_HEAD_DIM, (h + 1) * HG_HEAD_DIM)
                    ld = lambda ref: ref[0, pl.ds(r0, c), cs]
                    q, kk, bb = ld(q_ref), ld(k_ref), b_ref[di, pl.ds(r0, c), cs]
                    scores = _hgrn_scores(q, kk, bb, lev[rev], rev, base)
                    work.append((di, rev, h, r0, cs, q, kk, ld(v_ref), bb, scores, o_ref))
            for di, rev, h, r0, cs, q, kk, v, bb, scores, o_ref in work:
                o, st = _hgrn_output(scores, q, kk, v, bb, st_ref[di, h], rev)
                st_ref[di, h] = st
                o_ref[0, pl.ds(r0, c), cs] = o.astype(o_ref.dtype)
            return carry

        if base == 1:
            lax.fori_loop(0, n_sub, body, 0)
        else:
            for j in range(n_sub):
                body(j, 0)

    pl.when(fast)(lambda: run(HG_BASE))
    pl.when(jnp.logical_not(fast))(lambda: run(1))


def _hgrn(hq, hv, lff, kf, lfb, kb):
    b, s, w = hq.shape
    ns = s // HG_STEP
    fwd = pl.BlockSpec((1, HG_STEP, w), lambda i, j: (i, j, 0))
    bwd = pl.BlockSpec((1, HG_STEP, w), lambda i, j: (i, ns - 1 - j, 0))
    sds = jax.ShapeDtypeStruct((b, s, w), BF16)
    return pl.pallas_call(
        _hgrn_kernel,
        grid=(b, ns),
        in_specs=[fwd, fwd, fwd, fwd, bwd, bwd, bwd, bwd],
        out_specs=[fwd, bwd],
        out_shape=[sds, sds],
        scratch_shapes=[pltpu.VMEM((2, HG_HEADS, HG_HEAD_DIM, HG_HEAD_DIM), F32),
                        pltpu.VMEM((2, HG_STEP, w), F32)],
        compiler_params=pltpu.CompilerParams(
            dimension_semantics=("parallel", "arbitrary"), vmem_limit_bytes=V7X_VMEM_LIMIT),
        name="hgrn",
    )(hq, hv, lff, kf, hq, hv, lfb, kb)


def _rms(x, g):
    ms = jnp.mean(x * x, axis=-1, keepdims=True)
    return x * lax.rsqrt(ms + NORM_EPS) * g


def _out_kernel(a_ref, of_ref, ob_ref, hg_ref, x_ref, p_ref,
                gh_ref, wo_ref, gp_ref, wg_ref, wp_ref, gf_ref, o_ref):
    sub = x_ref.shape[0] // OUT_SPLIT
    tiles = [slice(u * sub, (u + 1) * sub) for u in range(OUT_SPLIT)]
    gh = gh_ref[...]

    def mixed(rows):
        r = of_ref[rows, :].astype(F32) + ob_ref[rows, :].astype(F32)
        parts = [a_ref[rows, :]]
        for h in range(HG_HEADS):
            cs = slice(h * HG_HEAD_DIM, (h + 1) * HG_HEAD_DIM)
            y = _rms(r[:, cs], gh[:, cs]) * hg_ref[rows, cs].astype(F32)
            parts.append(y.astype(BF16))
        return jnp.concatenate(parts, axis=-1)

    cats = [mixed(rows) for rows in tiles]
    pps = [jnp.dot(p_ref[rows, :].astype(BF16), wp_ref[...], preferred_element_type=F32)
           for rows in tiles]
    h1s = [x_ref[rows, :] + jnp.dot(cat, wo_ref[...], preferred_element_type=F32)
           for rows, cat in zip(tiles, cats)]
    gates = [_sigmoid(jnp.dot(_rms(h1, gp_ref[...]).astype(BF16), wg_ref[...],
                              preferred_element_type=F32)) for h1 in h1s]
    for rows, h1, gate, pp in zip(tiles, h1s, gates, pps):
        o_ref[rows, :] = _rms(h1 + gate * pp, gf_ref[...])


def _out(a, o_f, o_b, hg, x2, p2, hgrn_norm, w_out, ln_ple, w_pg, w_pp, ln_final):
    n = x2.shape[0]
    row = lambda w: pl.BlockSpec((OUT_TM, w), lambda i: (i, 0))
    full = lambda r, c: pl.BlockSpec((r, c), lambda i: (0, 0))
    return pl.pallas_call(
        _out_kernel,
        grid=(n // OUT_TM,),
        in_specs=[row(NA_WIDTH), row(HG_WIDTH), row(HG_WIDTH), row(HG_WIDTH),
                  row(D_MODEL), row(PLE_DIM),
                  full(1, HG_WIDTH), full(D_MODEL, D_MODEL), full(1, D_MODEL),
                  full(D_MODEL, D_MODEL), full(PLE_DIM, D_MODEL), full(1, D_MODEL)],
        out_specs=row(D_MODEL),
        out_shape=jax.ShapeDtypeStruct((n, D_MODEL), F32),
        compiler_params=pltpu.CompilerParams(
            dimension_semantics=("parallel",), vmem_limit_bytes=V7X_VMEM_LIMIT),
        name="out",
    )(a, o_f, o_b, hg, x2, p2, hgrn_norm, w_out, ln_ple, w_pg, w_pp, ln_final)


def kernel(x, p, ln_mix, w_in, rpb, lb_logits, attn_norm, hgrn_norm, w_out, ln_ple, w_pg, w_pp, ln_final):
    depth = p.shape[0]
    assert depth == 1 and lb_logits.shape[0] == 2
    b, s, d = x.shape
    n = b * s
    rows = s // GRID_W
    assert d == D_MODEL and n % PROJ_TM == 0 and n % OUT_TM == 0 and s % HG_STEP == 0
    assert rows % min(NA_ROWS_PER_ITER, rows) == 0 and rows >= NA_ROWS

    x2 = x.reshape(n, d)
    (aq, ak, av, ag, hq, lff, kf, lfb, kb, hv, hg) = _proj(
        x2, ln_mix[0:1], w_in[0].astype(BF16), lb_logits)

    r3 = lambda t: t.reshape(b, s, t.shape[-1])
    a = _attn(rpb[0].reshape(-1), r3(aq), r3(ak), r3(av), r3(ag), attn_norm[0:1])
    o_f, o_b = _hgrn(r3(hq), r3(hv), r3(lff), r3(kf), r3(lfb), r3(kb))

    out = _out(a.reshape(n, NA_WIDTH), o_f.reshape(n, HG_WIDTH), o_b.reshape(n, HG_WIDTH), hg,
               x2, p[0].reshape(n, PLE_DIM), hgrn_norm[0:1], w_out[0].astype(BF16),
               ln_ple[0:1], w_pg[0].astype(BF16), w_pp[0].astype(BF16), ln_final.reshape(1, d))
    return out.reshape(b, s, d)
```

```python
import jax
import jax.numpy as jnp
from jax import lax
from jax.experimental import pallas as pl
from jax.experimental.pallas import tpu as pltpu

F32 = jnp.float32
BF16 = jnp.bfloat16

D_MODEL = 1024
GRID_W = 64
NA_WIDTH = 512
HG_WIDTH = 512
NA_HEADS = 8
NA_HEAD_DIM = 64
NA_ROWS = 8
NA_COLS = 16
HG_HEAD_DIM = 128
HG_HEADS = 4
PLE_DIM = 256
NORM_EPS = 1e-6
N_GROUPS = 9
GROUP_WIDTH = NA_WIDTH
assert HG_WIDTH == GROUP_WIDTH
PROJ_TOTAL = N_GROUPS * GROUP_WIDTH

V7X_LANES = 128
V7X_VMEM_LIMIT = 56 * 1024 * 1024
MASK_VALUE = -1e30
LOG2E = 1.4426950408889634

PROJ_TM = 1024
PROJ_SPLIT = 8
OUT_TM = 1024
OUT_SPLIT = 2
HG_CHUNK = 128
HG_STEP = 2048
HG_BASE = HG_CHUNK
HG_MAX_LOG2_RANGE = 110.0
NA_PAIRS = NA_HEADS // 2
NA_ROWS_PER_ITER = 64


def _sigmoid(x):
    return 1.0 / (1.0 + jnp.exp(-x))


def _proj_kernel(x_ref, g_ref, w_ref, lbl_ref,
                 aq_ref, ak_ref, av_ref, ag_ref, hq_ref,
                 lff_ref, kf_ref, lfb_ref, kb_ref, hv_ref, hg_ref):
    l0 = lbl_ref[0]
    l1 = lbl_ref[1]
    m = jnp.maximum(l0, l1)
    e0 = jnp.exp(l0 - m)
    e1 = jnp.exp(l1 - m)
    lb = e0 / (e0 + e1)

    sub = x_ref.shape[0] // PROJ_SPLIT
    tiles = [slice(u * sub, (u + 1) * sub) for u in range(PROJ_SPLIT)]
    hns = []
    for rows in tiles:
        x = x_ref[rows, :]
        ms = jnp.mean(x * x, axis=-1, keepdims=True)
        hns.append((x * lax.rsqrt(ms + NORM_EPS) * g_ref[...]).astype(BF16))

    def silu(t):
        return t * _sigmoid(t)

    def forget(lbj):
        def act(t):
            f = lbj + (1.0 - lbj) * _sigmoid(t)
            return jnp.log(f), 1.0 - f
        return act

    scale_q = lambda t: t * (NA_HEAD_DIM ** -0.5 * LOG2E)
    ident = lambda t: t
    groups = ((scale_q, (aq_ref,)), (ident, (ak_ref,)), (ident, (av_ref,)), (silu, (ag_ref,)),
              (silu, (hq_ref,)), (forget(lb[0:1, :]), (lff_ref, kf_ref)),
              (forget(lb[1:2, :]), (lfb_ref, kb_ref)), (ident, (hv_ref,)), (silu, (hg_ref,)))
    for j, (act, out_refs) in enumerate(groups):
        w = w_ref[:, j * GROUP_WIDTH:(j + 1) * GROUP_WIDTH]
        for rows, hn in zip(tiles, hns):
            res = act(jnp.dot(hn, w, preferred_element_type=F32))
            for ref, val in zip(out_refs, res if isinstance(res, tuple) else (res,)):
                ref[rows, :] = val.astype(BF16)


def _proj(x2, ln_mix, w_in, lb_logits):
    n = x2.shape[0]
    out_sds = jax.ShapeDtypeStruct((n, GROUP_WIDTH), BF16)
    row_spec = pl.BlockSpec((PROJ_TM, GROUP_WIDTH), lambda i: (i, 0))
    return pl.pallas_call(
        _proj_kernel,
        grid=(n // PROJ_TM,),
        in_specs=[
            pl.BlockSpec((PROJ_TM, D_MODEL), lambda i: (i, 0)),
            pl.BlockSpec((1, D_MODEL), lambda i: (0, 0)),
            pl.BlockSpec((D_MODEL, PROJ_TOTAL), lambda i: (0, 0), pipeline_mode=pl.Buffered(1)),
            pl.BlockSpec((2, 2, HG_WIDTH), lambda i: (0, 0, 0)),
        ],
        out_specs=[row_spec] * 11,
        out_shape=[out_sds] * 11,
        compiler_params=pltpu.CompilerParams(
            dimension_semantics=("parallel",), vmem_limit_bytes=V7X_VMEM_LIMIT),
        name="proj",
    )(x2, ln_mix, w_in, lb_logits)


def _attn_build_table(rpb_ref, tab_ref, pair):
    kh = tab_ref.shape[0]
    n_dr, n_dc = 2 * NA_ROWS - 1, 2 * NA_COLS - 1
    c_idx = lax.broadcasted_iota(jnp.int32, (GRID_W, V7X_LANES), 0)
    kc_idx = lax.broadcasted_iota(jnp.int32, (GRID_W, V7X_LANES), 1) % GRID_W
    start = jnp.clip(c_idx - NA_COLS // 2, 0, GRID_W - NA_COLS)
    valid = (kc_idx >= start) & (kc_idx < start + NA_COLS)
    diff = kc_idx - c_idx + NA_COLS - 1
    low_half = lax.broadcasted_iota(jnp.int32, (GRID_W, V7X_LANES), 1) < GRID_W
    for j in range(2):
        head = 2 * pair + j
        toeplitz = {}
        for dr in range(NA_ROWS - kh, NA_ROWS - 1 + kh):
            t = jnp.full((GRID_W, V7X_LANES), MASK_VALUE, F32)
            for dc in range(n_dc):
                t = jnp.where(diff == dc, rpb_ref[(head * n_dr + dr) * n_dc + dc] * LOG2E, t)
            toeplitz[dr] = jnp.where(valid, t, MASK_VALUE)
        for d in range(kh):
            for i in range(0, kh, 2):
                blk = jnp.where(low_half, toeplitz[i - d + NA_ROWS - 1], toeplitz[i + 1 - d + NA_ROWS - 1])
                tab_ref[d, j * GRID_W:(j + 1) * GRID_W, i * GRID_W:(i + 2) * GRID_W] = blk


def _attn_kernel(rpb_ref, q_ref, k_ref, v_ref, sg_ref, g_ref, o_ref, tab_ref):
    rows = q_ref.shape[1] // GRID_W
    kh = min(NA_ROWS, rows)
    nkeys = kh * GRID_W
    lane = lax.broadcasted_iota(jnp.int32, (GRID_W, V7X_LANES), 1)
    first = lane < NA_HEAD_DIM
    m0 = first.astype(F32).astype(BF16)
    m1 = (1.0 - first.astype(F32)).astype(BF16)
    gain = g_ref[...]

    @pl.when(pl.program_id(1) == 0)
    def _():
        _attn_build_table(rpb_ref, tab_ref, pl.program_id(0))

    def scores(r):
        rs = jnp.clip(r - kh // 2, 0, rows - kh)
        q0 = pl.multiple_of(r * GRID_W, GRID_W)
        k0 = pl.multiple_of(rs * GRID_W, GRID_W)
        q = q_ref[0, pl.ds(q0, GRID_W), :]
        kk = k_ref[0, pl.ds(k0, nkeys), :]
        qm = jnp.concatenate([m0 * q, m1 * q], axis=0)
        s = lax.dot_general(qm, kk, (((1,), (1,)), ((), ())), preferred_element_type=F32)
        return q0, k0, s + tab_ref[r - rs]

    def softmax(s):
        return jnp.exp2(s - jnp.max(s, axis=-1, keepdims=True)).astype(BF16)

    ones = jnp.ones((nkeys, V7X_LANES), BF16)

    def weighted_values(k0, p):
        vals = jnp.concatenate([v_ref[0, pl.ds(k0, nkeys), :], ones], axis=1)
        return jnp.dot(p, vals, preferred_element_type=F32)

    def finish(q0, o2):
        o2 = o2[:, :V7X_LANES] / o2[:, V7X_LANES:]
        o = jnp.where(first, o2[:GRID_W], o2[GRID_W:])
        sq = o * o
        ms0 = jnp.sum(jnp.where(first, sq, 0.0), axis=-1, keepdims=True)
        ms1 = jnp.sum(jnp.where(first, 0.0, sq), axis=-1, keepdims=True)
        ms = jnp.where(first, ms0, ms1) * (1.0 / NA_HEAD_DIM)
        y = o * lax.rsqrt(ms + NORM_EPS) * gain
        y = y * sg_ref[0, pl.ds(q0, GRID_W), :].astype(F32)
        o_ref[0, pl.ds(q0, GRID_W), :] = y.astype(BF16)

    per_iter = min(NA_ROWS_PER_ITER, rows)

    def body(i, carry):
        rs_ = [scores(i * per_iter + u) for u in range(per_iter)]
        ps = [softmax(s) for _, _, s in rs_]
        os_ = [weighted_values(k0, p) for (_, k0, _), p in zip(rs_, ps)]
        for (q0, _, _), o2 in zip(rs_, os_):
            finish(q0, o2)
        return carry

    lax.fori_loop(0, rows // per_iter, body, 0)


def _attn(rpb_flat, aq, ak, av, ag, attn_norm):
    b, s, _ = aq.shape
    kh = min(NA_ROWS, s // GRID_W)
    assert kh % 2 == 0
    blk = pl.BlockSpec((1, s, V7X_LANES), lambda hp, i: (i, 0, hp))
    return pl.pallas_call(
        _attn_kernel,
        grid=(NA_PAIRS, b),
        in_specs=[pl.BlockSpec(memory_space=pltpu.SMEM), blk, blk, blk, blk,
                  pl.BlockSpec((1, V7X_LANES), lambda hp, i: (0, hp))],
        out_specs=blk,
        out_shape=jax.ShapeDtypeStruct((b, s, NA_WIDTH), BF16),
        scratch_shapes=[pltpu.VMEM((kh, 2 * GRID_W, kh * GRID_W), F32)],
        compiler_params=pltpu.CompilerParams(
            dimension_semantics=("parallel", "arbitrary"), vmem_limit_bytes=V7X_VMEM_LIMIT),
        name="attn",
    )(rpb_flat, aq, ak, av, ag, attn_norm)


def _block_ref(b, n, rev):
    c, w = b.shape
    pos = n if rev else n - 1
    if 2 * n >= 8:
        b3 = b.reshape(c // (2 * n), 2 * n, w)
        return jnp.broadcast_to(b3[:, pos:pos + 1, :], b3.shape).reshape(c, w)
    row = lax.broadcasted_iota(jnp.int32, b.shape, 0)
    if n == 1:
        if rev:
            return jnp.where(row % 2 == 0, pltpu.roll(b, c - 1, 0), b)
        return jnp.where(row % 2 == 1, pltpu.roll(b, 1, 0), b)
    assert n == 2
    if rev:
        c1 = jnp.where(row % 2 == 1, pltpu.roll(b, 1, 0), b)
        return jnp.where(row % 4 < 2, pltpu.roll(c1, c - 2, 0), c1)
    c1 = jnp.where(row % 2 == 0, pltpu.roll(b, c - 1, 0), b)
    return jnp.where(row % 4 >= 2, pltpu.roll(c1, 2, 0), c1)


def _level_decay(b, n, rev):
    c, w = b.shape
    if n < 8:
        return jnp.exp2(-jnp.abs(b - _block_ref(b, n, rev)))
    b3 = b.reshape(c // (2 * n), 2 * n, w)
    lo, hi = b3[:, :n], b3[:, n:]
    if rev:
        ref = b3[:, n:n + 1]
        parts = [jnp.exp2(lo - ref), jnp.exp2(ref - hi)]
    else:
        ref = b3[:, n - 1:n]
        parts = [jnp.exp2(ref - lo), jnp.exp2(hi - ref)]
    return jnp.concatenate(parts, axis=1).reshape(c, w)


def _base_ref(b3, rev):
    base = b3.shape[1]
    pos = base // 2 if rev else base // 2 - 1
    return b3[:, pos:pos + 1]


def _level_ids(c, rev, base):
    t_idx = lax.broadcasted_iota(jnp.int32, (c, c), 0)
    s_idx = lax.broadcasted_iota(jnp.int32, (c, c), 1)
    x = t_idx ^ s_idx
    lev = jnp.ones((c, c), jnp.int32)
    n = 2 * base
    while n < c:
        lev = lev + (x >= n).astype(jnp.int32)
        n *= 2
    before = (t_idx < s_idx) if rev else (t_idx > s_idx)
    lev = jnp.where(before & (x >= base), lev, -1)
    return jnp.where((x < base) & (before | (t_idx == s_idx)), 0, lev)


def _hgrn_scores(q, kk, b, lev, rev, base):
    c, w = q.shape
    nt = (((1,), (1,)), ((), ()))

    if base == 1:
        scores = lax.dot_general(q, kk, nt, preferred_element_type=F32)
    else:
        b3 = b.reshape(c // base, base, w)
        d = (b3 - _base_ref(b3, rev)).reshape(c, w)
        scores = lax.dot_general(jnp.exp2(d).astype(BF16) * q, jnp.exp2(-d).astype(BF16) * kk, nt,
                                 preferred_element_type=F32)
    scores = jnp.where(lev == 0, scores, 0.0)
    n, li = base, 1
    while n < c:
        e = _level_decay(b, n, rev).astype(BF16)
        sl = lax.dot_general(e * q, e * kk, nt, preferred_element_type=F32)
        scores = jnp.where(lev == li, sl, scores)
        n, li = 2 * n, li + 1
    return scores.astype(BF16)


def _hgrn_output(scores, q, kk, v, b, st, rev):
    c, w = q.shape
    btot = b[0:1, :] if rev else b[c - 1:c, :]
    lhs = jnp.concatenate([scores, jnp.exp2(b).astype(BF16) * q], axis=1)
    rhs = jnp.concatenate([v, st.astype(BF16)], axis=0)
    o = jnp.dot(lhs, rhs, preferred_element_type=F32)

    kd = jnp.exp2(btot - b).astype(BF16) * kk
    upd = lax.dot_general(kd, v, (((0,), (0,)), ((), ())), preferred_element_type=F32)
    keep = jnp.broadcast_to(jnp.exp2(btot), (w, w)).T
    return o, st * keep + upd


def _hgrn_kernel(qf_ref, vf_ref, lff_ref, kf_ref, qb_ref, vb_ref, lfb_ref, kb_ref,
                 of_ref, ob_ref, st_ref, b_ref):
    @pl.when(pl.program_id(1) == 0)
    def _():
        st_ref[...] = jnp.zeros_like(st_ref)

    c = HG_CHUNK
    n_sub = qf_ref.shape[1] // c
    t_idx = lax.broadcasted_iota(jnp.int32, (c, c), 0)
    u_idx = lax.broadcasted_iota(jnp.int32, (c, c), 1)
    dirs = ((0, False, qf_ref, kf_ref, vf_ref, lff_ref, of_ref),
            (1, True, qb_ref, kb_ref, vb_ref, lfb_ref, ob_ref))

    widest = jnp.zeros((c // HG_BASE, 1, lff_ref.shape[2]), F32)
    for di, rev, _, _, _, lf_ref, _ in dirs:
        tri = ((u_idx >= t_idx) if rev else (u_idx <= t_idx)).astype(F32).astype(BF16)
        for j in range(n_sub):
            rows = slice(j * c, (j + 1) * c)
            bj = jnp.dot(tri, lf_ref[0, rows, :], preferred_element_type=F32) * LOG2E
            b_ref[di, rows, :] = bj
            b3 = bj.reshape(c // HG_BASE, HG_BASE, bj.shape[1])
            ref = _base_ref(b3, rev)
            widest = jnp.maximum(widest, jnp.maximum(jnp.abs(b3[:, 0:1] - ref),
                                                     jnp.abs(b3[:, HG_BASE - 1:HG_BASE] - ref)))
    fast = jnp.max(widest) <= HG_MAX_LOG2_RANGE

    def run(base):
        lev = {rev: _level_ids(c, rev, base) for rev in (False, True)}

        def body(j, carry):
            for di, rev, q_ref, k_ref, v_ref, _, o_ref in dirs:
                work = []
                r0 = ((n_sub - 1 - j) if rev else j) * c
                if not isinstance(j, int):
                    r0 = pl.multiple_of(r0, c)
                for h in range(HG_HEADS):
                    cs = slice(h * HG_HEAD_DIM, (h + 1) * HG_HEAD_DIM)
                    ld = lambda ref: ref[0, pl.ds(r0, c), cs]
                    q, kk, bb = ld(q_ref), ld(k_ref), b_ref[di, pl.ds(r0, c), cs]
                    scores = _hgrn_scores(q, kk, bb, lev[rev], rev, base)
                    work.append((h, cs, q, kk, ld(v_ref), bb, scores))
                for h, cs, q, kk, v, bb, scores in work:
                    o, st = _hgrn_output(scores, q, kk, v, bb, st_ref[di, h], rev)
                    st_ref[di, h] = st
                    o_ref[0, pl.ds(r0, c), cs] = o.astype(o_ref.dtype)
            return carry

        if base == 1:
            lax.fori_loop(0, n_sub, body, 0)
        else:
            for j in range(n_sub):
                body(j, 0)

    pl.when(fast)(lambda: run(HG_BASE))
    pl.when(jnp.logical_not(fast))(lambda: run(1))


def _hgrn(hq, hv, lff, kf, lfb, kb):
    b, s, w = hq.shape
    ns = s // HG_STEP
    fwd = pl.BlockSpec((1, HG_STEP, w), lambda i, j: (i, j, 0))
    bwd = pl.BlockSpec((1, HG_STEP, w), lambda i, j: (i, ns - 1 - j, 0))
    sds = jax.ShapeDtypeStruct((b, s, w), BF16)
    return pl.pallas_call(
        _hgrn_kernel,
        grid=(b, ns),
        in_specs=[fwd, fwd, fwd, fwd, bwd, bwd, bwd, bwd],
        out_specs=[fwd, bwd],
        out_shape=[sds, sds],
        scratch_shapes=[pltpu.VMEM((2, HG_HEADS, HG_HEAD_DIM, HG_HEAD_DIM), F32),
                        pltpu.VMEM((2, HG_STEP, w), F32)],
        compiler_params=pltpu.CompilerParams(
            dimension_semantics=("parallel", "arbitrary"), vmem_limit_bytes=V7X_VMEM_LIMIT),
        name="hgrn",
    )(hq, hv, lff, kf, hq, hv, lfb, kb)


def _rms(x, g):
    ms = jnp.mean(x * x, axis=-1, keepdims=True)
    return x * lax.rsqrt(ms + NORM_EPS) * g


def _out_kernel(a_ref, of_ref, ob_ref, hg_ref, x_ref, p_ref,
                gh_ref, wo_ref, gp_ref, wg_ref, wp_ref, gf_ref, o_ref):
    sub = x_ref.shape[0] // OUT_SPLIT
    tiles = [slice(u * sub, (u + 1) * sub) for u in range(OUT_SPLIT)]
    gh = gh_ref[...]

    def mixed(rows):
        r = of_ref[rows, :].astype(F32) + ob_ref[rows, :].astype(F32)
        parts = [a_ref[rows, :]]
        for h in range(HG_HEADS):
            cs = slice(h * HG_HEAD_DIM, (h + 1) * HG_HEAD_DIM)
            y = _rms(r[:, cs], gh[:, cs]) * hg_ref[rows, cs].astype(F32)
            parts.append(y.astype(BF16))
        return jnp.concatenate(parts, axis=-1)

    cats = [mixed(rows) for rows in tiles]
    pps = [jnp.dot(p_ref[rows, :].astype(BF16), wp_ref[...], preferred_element_type=F32)
           for rows in tiles]
    h1s = [x_ref[rows, :] + jnp.dot(cat, wo_ref[...], preferred_element_type=F32)
           for rows, cat in zip(tiles, cats)]
    gates = [_sigmoid(jnp.dot(_rms(h1, gp_ref[...]).astype(BF16), wg_ref[...],
                              preferred_element_type=F32)) for h1 in h1s]
    for rows, h1, gate, pp in zip(tiles, h1s, gates, pps):
        o_ref[rows, :] = _rms(h1 + gate * pp, gf_ref[...])


def _out(a, o_f, o_b, hg, x2, p2, hgrn_norm, w_out, ln_ple, w_pg, w_pp, ln_final):
    n = x2.shape[0]
    row = lambda w: pl.BlockSpec((OUT_TM, w), lambda i: (i, 0))
    full = lambda r, c: pl.BlockSpec((r, c), lambda i: (0, 0))
    return pl.pallas_call(
        _out_kernel,
        grid=(n // OUT_TM,),
        in_specs=[row(NA_WIDTH), row(HG_WIDTH), row(HG_WIDTH), row(HG_WIDTH),
                  row(D_MODEL), row(PLE_DIM),
                  full(1, HG_WIDTH), full(D_MODEL, D_MODEL), full(1, D_MODEL),
                  full(D_MODEL, D_MODEL), full(PLE_DIM, D_MODEL), full(1, D_MODEL)],
        out_specs=row(D_MODEL),
        out_shape=jax.ShapeDtypeStruct((n, D_MODEL), F32),
        compiler_params=pltpu.CompilerParams(
            dimension_semantics=("parallel",), vmem_limit_bytes=V7X_VMEM_LIMIT),
        name="out",
    )(a, o_f, o_b, hg, x2, p2, hgrn_norm, w_out, ln_ple, w_pg, w_pp, ln_final)


def kernel(x, p, ln_mix, w_in, rpb, lb_logits, attn_norm, hgrn_norm, w_out, ln_ple, w_pg, w_pp, ln_final):
    depth = p.shape[0]
    assert depth == 1 and lb_logits.shape[0] == 2
    b, s, d = x.shape
    n = b * s
    rows = s // GRID_W
    assert d == D_MODEL and n % PROJ_TM == 0 and n % OUT_TM == 0 and s % HG_STEP == 0
    assert rows % min(NA_ROWS_PER_ITER, rows) == 0 and rows >= NA_ROWS

    x2 = x.reshape(n, d)
    (aq, ak, av, ag, hq, lff, kf, lfb, kb, hv, hg) = _proj(
        x2, ln_mix[0:1], w_in[0].astype(BF16), lb_logits)

    r3 = lambda t: t.reshape(b, s, t.shape[-1])
    a = _attn(rpb[0].reshape(-1), r3(aq), r3(ak), r3(av), r3(ag), attn_norm[0:1])
    o_f, o_b = _hgrn(r3(hq), r3(hv), r3(lff), r3(kf), r3(lfb), r3(kb))

    out = _out(a.reshape(n, NA_WIDTH), o_f.reshape(n, HG_WIDTH), o_b.reshape(n, HG_WIDTH), hg,
               x2, p[0].reshape(n, PLE_DIM), hgrn_norm[0:1], w_out[0].astype(BF16),
               ln_ple[0:1], w_pg[0].astype(BF16), w_pp[0].astype(BF16), ln_final.reshape(1, d))
    return out.reshape(b, s, d)
```

```python
import jax
import jax.numpy as jnp
from jax import lax
from jax.experimental import pallas as pl
from jax.experimental.pallas import tpu as pltpu

F32 = jnp.float32
BF16 = jnp.bfloat16

D_MODEL = 1024
GRID_W = 64
NA_WIDTH = 512
HG_WIDTH = 512
NA_HEADS = 8
NA_HEAD_DIM = 64
NA_ROWS = 8
NA_COLS = 16
HG_HEAD_DIM = 128
HG_HEADS = 4
PLE_DIM = 256
NORM_EPS = 1e-6
N_GROUPS = 9
GROUP_WIDTH = NA_WIDTH
assert HG_WIDTH == GROUP_WIDTH
PROJ_TOTAL = N_GROUPS * GROUP_WIDTH

V7X_LANES = 128
V7X_VMEM_LIMIT = 56 * 1024 * 1024
MASK_VALUE = -1e30
LOG2E = 1.4426950408889634

PROJ_TM = 1024
PROJ_SPLIT = 8
OUT_TM = 1024
OUT_SPLIT = 2
HG_CHUNK = 128
HG_STEP = 2048
HG_BASE = HG_CHUNK
HG_MAX_LOG2_RANGE = 110.0
NA_PAIRS = NA_HEADS // 2
NA_ROWS_PER_ITER = 64


def _sigmoid(x):
    return 1.0 / (1.0 + jnp.exp(-x))


def _proj_kernel(x_ref, g_ref, w_ref, lbl_ref,
                 aq_ref, ak_ref, av_ref, ag_ref, hq_ref,
                 lff_ref, kf_ref, lfb_ref, kb_ref, hv_ref, hg_ref):
    l0 = lbl_ref[0]
    l1 = lbl_ref[1]
    m = jnp.maximum(l0, l1)
    e0 = jnp.exp(l0 - m)
    e1 = jnp.exp(l1 - m)
    lb = e0 / (e0 + e1)

    sub = x_ref.shape[0] // PROJ_SPLIT
    tiles = [slice(u * sub, (u + 1) * sub) for u in range(PROJ_SPLIT)]
    hns = []
    for rows in tiles:
        x = x_ref[rows, :]
        ms = jnp.mean(x * x, axis=-1, keepdims=True)
        hns.append((x * lax.rsqrt(ms + NORM_EPS) * g_ref[...]).astype(BF16))

    def silu(t):
        return t * _sigmoid(t)

    def forget(lbj):
        def act(t):
            f = lbj + (1.0 - lbj) * _sigmoid(t)
            return jnp.log(f), 1.0 - f
        return act

    scale_q = lambda t: t * (NA_HEAD_DIM ** -0.5 * LOG2E)
    ident = lambda t: t
    groups = ((scale_q, (aq_ref,)), (ident, (ak_ref,)), (ident, (av_ref,)), (silu, (ag_ref,)),
              (silu, (hq_ref,)), (forget(lb[0:1, :]), (lff_ref, kf_ref)),
              (forget(lb[1:2, :]), (lfb_ref, kb_ref)), (ident, (hv_ref,)), (silu, (hg_ref,)))
    for j, (act, out_refs) in enumerate(groups):
        w = w_ref[:, j * GROUP_WIDTH:(j + 1) * GROUP_WIDTH]
        for rows, hn in zip(tiles, hns):
            res = act(jnp.dot(hn, w, preferred_element_type=F32))
            for ref, val in zip(out_refs, res if isinstance(res, tuple) else (res,)):
                ref[rows, :] = val.astype(BF16)


def _proj(x2, ln_mix, w_in, lb_logits):
    n = x2.shape[0]
    out_sds = jax.ShapeDtypeStruct((n, GROUP_WIDTH), BF16)
    row_spec = pl.BlockSpec((PROJ_TM, GROUP_WIDTH), lambda i: (i, 0))
    return pl.pallas_call(
        _proj_kernel,
        grid=(n // PROJ_TM,),
        in_specs=[
            pl.BlockSpec((PROJ_TM, D_MODEL), lambda i: (i, 0)),
            pl.BlockSpec((1, D_MODEL), lambda i: (0, 0)),
            pl.BlockSpec((D_MODEL, PROJ_TOTAL), lambda i: (0, 0), pipeline_mode=pl.Buffered(1)),
            pl.BlockSpec((2, 2, HG_WIDTH), lambda i: (0, 0, 0)),
        ],
        out_specs=[row_spec] * 11,
        out_shape=[out_sds] * 11,
        compiler_params=pltpu.CompilerParams(
            dimension_semantics=("parallel",), vmem_limit_bytes=V7X_VMEM_LIMIT),
        name="proj",
    )(x2, ln_mix, w_in, lb_logits)


def _attn_build_table(rpb_ref, tab_ref, pair):
    kh = tab_ref.shape[0]
    n_dr, n_dc = 2 * NA_ROWS - 1, 2 * NA_COLS - 1
    c_idx = lax.broadcasted_iota(jnp.int32, (GRID_W, V7X_LANES), 0)
    kc_idx = lax.broadcasted_iota(jnp.int32, (GRID_W, V7X_LANES), 1) % GRID_W
    start = jnp.clip(c_idx - NA_COLS // 2, 0, GRID_W - NA_COLS)
    valid = (kc_idx >= start) & (kc_idx < start + NA_COLS)
    diff = kc_idx - c_idx + NA_COLS - 1
    low_half = lax.broadcasted_iota(jnp.int32, (GRID_W, V7X_LANES), 1) < GRID_W
    for j in range(2):
        head = 2 * pair + j
        toeplitz = {}
        for dr in range(NA_ROWS - kh, NA_ROWS - 1 + kh):
            t = jnp.full((GRID_W, V7X_LANES), MASK_VALUE, F32)
            for dc in range(n_dc):
                t = jnp.where(diff == dc, rpb_ref[(head * n_dr + dr) * n_dc + dc] * LOG2E, t)
            toeplitz[dr] = jnp.where(valid, t, MASK_VALUE)
        for d in range(kh):
            for i in range(0, kh, 2):
                blk = jnp.where(low_half, toeplitz[i - d + NA_ROWS - 1], toeplitz[i + 1 - d + NA_ROWS - 1])
                tab_ref[d, j * GRID_W:(j + 1) * GRID_W, i * GRID_W:(i + 2) * GRID_W] = blk


def _attn_kernel(rpb_ref, q_ref, k_ref, v_ref, sg_ref, g_ref, o_ref, tab_ref):
    rows = q_ref.shape[1] // GRID_W
    kh = min(NA_ROWS, rows)
    nkeys = kh * GRID_W
    lane = lax.broadcasted_iota(jnp.int32, (GRID_W, V7X_LANES), 1)
    first = lane < NA_HEAD_DIM
    m0 = first.astype(F32).astype(BF16)
    m1 = (1.0 - first.astype(F32)).astype(BF16)
    gain = g_ref[...]

    @pl.when(pl.program_id(1) == 0)
    def _():
        _attn_build_table(rpb_ref, tab_ref, pl.program_id(0))

    def scores(r):
        rs = jnp.clip(r - kh // 2, 0, rows - kh)
        q0 = pl.multiple_of(r * GRID_W, GRID_W)
        k0 = pl.multiple_of(rs * GRID_W, GRID_W)
        q = q_ref[0, pl.ds(q0, GRID_W), :]
        kk = k_ref[0, pl.ds(k0, nkeys), :]
        qm = jnp.concatenate([m0 * q, m1 * q], axis=0)
        s = lax.dot_general(qm, kk, (((1,), (1,)), ((), ())), preferred_element_type=F32)
        return q0, k0, s + tab_ref[r - rs]

    def softmax(s):
        return jnp.exp2(s - jnp.max(s, axis=-1, keepdims=True)).astype(BF16)

    ones = jnp.ones((nkeys, V7X_LANES), BF16)

    def weighted_values(k0, p):
        vals = jnp.concatenate([v_ref[0, pl.ds(k0, nkeys), :], ones], axis=1)
        return jnp.dot(p, vals, preferred_element_type=F32)

    def finish(q0, o2):
        o2 = o2[:, :V7X_LANES] / o2[:, V7X_LANES:]
        o = jnp.where(first, o2[:GRID_W], o2[GRID_W:])
        sq = o * o
        ms0 = jnp.sum(jnp.where(first, sq, 0.0), axis=-1, keepdims=True)
        ms1 = jnp.sum(jnp.where(first, 0.0, sq), axis=-1, keepdims=True)
        ms = jnp.where(first, ms0, ms1) * (1.0 / NA_HEAD_DIM)
        y = o * lax.rsqrt(ms + NORM_EPS) * gain
        y = y * sg_ref[0, pl.ds(q0, GRID_W), :].astype(F32)
        o_ref[0, pl.ds(q0, GRID_W), :] = y.astype(BF16)

    per_iter = min(NA_ROWS_PER_ITER, rows)

    def body(i, carry):
        rs_ = [scores(i * per_iter + u) for u in range(per_iter)]
        ps = [softmax(s) for _, _, s in rs_]
        os_ = [weighted_values(k0, p) for (_, k0, _), p in zip(rs_, ps)]
        for (q0, _, _), o2 in zip(rs_, os_):
            finish(q0, o2)
        return carry

    lax.fori_loop(0, rows // per_iter, body, 0)


def _attn(rpb_flat, aq, ak, av, ag, attn_norm):
    b, s, _ = aq.shape
    kh = min(NA_ROWS, s // GRID_W)
    assert kh % 2 == 0
    blk = pl.BlockSpec((1, s, V7X_LANES), lambda hp, i: (i, 0, hp))
    return pl.pallas_call(
        _attn_kernel,
        grid=(NA_PAIRS, b),
        in_specs=[pl.BlockSpec(memory_space=pltpu.SMEM), blk, blk, blk, blk,
                  pl.BlockSpec((1, V7X_LANES), lambda hp, i: (0, hp))],
        out_specs=blk,
        out_shape=jax.ShapeDtypeStruct((b, s, NA_WIDTH), BF16),
        scratch_shapes=[pltpu.VMEM((kh, 2 * GRID_W, kh * GRID_W), F32)],
        compiler_params=pltpu.CompilerParams(
            dimension_semantics=("parallel", "arbitrary"), vmem_limit_bytes=V7X_VMEM_LIMIT),
        name="attn",
    )(rpb_flat, aq, ak, av, ag, attn_norm)


def _block_ref(b, n, rev):
    c, w = b.shape
    pos = n if rev else n - 1
    if 2 * n >= 8:
        b3 = b.reshape(c // (2 * n), 2 * n, w)
        return jnp.broadcast_to(b3[:, pos:pos + 1, :], b3.shape).reshape(c, w)
    row = lax.broadcasted_iota(jnp.int32, b.shape, 0)
    if n == 1:
        if rev:
            return jnp.where(row % 2 == 0, pltpu.roll(b, c - 1, 0), b)
        return jnp.where(row % 2 == 1, pltpu.roll(b, 1, 0), b)
    assert n == 2
    if rev:
        c1 = jnp.where(row % 2 == 1, pltpu.roll(b, 1, 0), b)
        return jnp.where(row % 4 < 2, pltpu.roll(c1, c - 2, 0), c1)
    c1 = jnp.where(row % 2 == 0, pltpu.roll(b, c - 1, 0), b)
    return jnp.where(row % 4 >= 2, pltpu.roll(c1, 2, 0), c1)


def _level_decay(b, n, rev):
    c, w = b.shape
    if n < 8:
        return jnp.exp2(-jnp.abs(b - _block_ref(b, n, rev)))
    b3 = b.reshape(c // (2 * n), 2 * n, w)
    lo, hi = b3[:, :n], b3[:, n:]
    if rev:
        ref = b3[:, n:n + 1]
        parts = [jnp.exp2(lo - ref), jnp.exp2(ref - hi)]
    else:
        ref = b3[:, n - 1:n]
        parts = [jnp.exp2(ref - lo), jnp.exp2(hi - ref)]
    return jnp.concatenate(parts, axis=1).reshape(c, w)


def _base_ref(b3, rev):
    base = b3.shape[1]
    pos = base // 2 if rev else base // 2 - 1
    return b3[:, pos:pos + 1]


def _level_ids(c, rev, base):
    t_idx = lax.broadcasted_iota(jnp.int32, (c, c), 0)
    s_idx = lax.broadcasted_iota(jnp.int32, (c, c), 1)
    x = t_idx ^ s_idx
    lev = jnp.ones((c, c), jnp.int32)
    n = 2 * base
    while n < c:
        lev = lev + (x >= n).astype(jnp.int32)
        n *= 2
    before = (t_idx < s_idx) if rev else (t_idx > s_idx)
    lev = jnp.where(before & (x >= base), lev, -1)
    return jnp.where((x < base) & (before | (t_idx == s_idx)), 0, lev)


def _hgrn_scores(q, kk, b, lev, rev, base):
    c, w = q.shape
    nt = (((1,), (1,)), ((), ()))

    if base == 1:
        scores, growth, decay = lax.dot_general(q, kk, nt, preferred_element_type=F32), None, None
    else:
        b3 = b.reshape(c // base, base, w)
        d = (b3 - _base_ref(b3, rev)).reshape(c, w)
        growth, decay = jnp.exp2(-d).astype(BF16), jnp.exp2(d).astype(BF16)
        scores = lax.dot_general(decay * q, growth * kk, nt, preferred_element_type=F32)
    scores = jnp.where(lev == 0, scores, 0.0)
    n, li = base, 1
    while n < c:
        e = _level_decay(b, n, rev).astype(BF16)
        sl = lax.dot_general(e * q, e * kk, nt, preferred_element_type=F32)
        scores = jnp.where(lev == li, sl, scores)
        n, li = 2 * n, li + 1
    return scores.astype(BF16), growth, decay


def _hgrn_output(scores, q, kk, v, b, st, rev, growth, decay):
    c, w = q.shape
    btot = b[0:1, :] if rev else b[c - 1:c, :]
    if decay is None:
        from_start, to_end = jnp.exp2(b).astype(BF16), jnp.exp2(btot - b).astype(BF16)
    else:
        bmid = _base_ref(b.reshape(1, c, w), rev).reshape(1, w)
        from_start = decay * jnp.exp2(bmid).astype(BF16)
        to_end = growth * jnp.exp2(btot - bmid).astype(BF16)
    lhs = jnp.concatenate([scores, from_start * q], axis=1)
    rhs = jnp.concatenate([v, st.astype(BF16)], axis=0)
    o = jnp.dot(lhs, rhs, preferred_element_type=F32)

    kd = to_end * kk
    upd = lax.dot_general(kd, v, (((0,), (0,)), ((), ())), preferred_element_type=F32)
    keep = jnp.broadcast_to(jnp.exp2(btot), (w, w)).T
    return o, st * keep + upd


def _hgrn_kernel(qf_ref, vf_ref, lff_ref, kf_ref, qb_ref, vb_ref, lfb_ref, kb_ref,
                 of_ref, ob_ref, st_ref, b_ref):
    @pl.when(pl.program_id(1) == 0)
    def _():
        st_ref[...] = jnp.zeros_like(st_ref)

    c = HG_CHUNK
    n_sub = qf_ref.shape[1] // c
    t_idx = lax.broadcasted_iota(jnp.int32, (c, c), 0)
    u_idx = lax.broadcasted_iota(jnp.int32, (c, c), 1)
    dirs = ((0, False, qf_ref, kf_ref, vf_ref, lff_ref, of_ref),
            (1, True, qb_ref, kb_ref, vb_ref, lfb_ref, ob_ref))

    widest = jnp.zeros((c // HG_BASE, 1, lff_ref.shape[2]), F32)
    for di, rev, _, _, _, lf_ref, _ in dirs:
        tri = ((u_idx >= t_idx) if rev else (u_idx <= t_idx)).astype(F32).astype(BF16)
        for j in range(n_sub):
            rows = slice(j * c, (j + 1) * c)
            bj = jnp.dot(tri, lf_ref[0, rows, :], preferred_element_type=F32) * LOG2E
            b_ref[di, rows, :] = bj
            b3 = bj.reshape(c // HG_BASE, HG_BASE, bj.shape[1])
            ref = _base_ref(b3, rev)
            widest = jnp.maximum(widest, jnp.maximum(jnp.abs(b3[:, 0:1] - ref),
                                                     jnp.abs(b3[:, HG_BASE - 1:HG_BASE] - ref)))
    fast = jnp.max(widest) <= HG_MAX_LOG2_RANGE

    def run(base):
        lev = {rev: _level_ids(c, rev, base) for rev in (False, True)}

        def body(j, carry):
            work = []
            for di, rev, q_ref, k_ref, v_ref, _, o_ref in dirs:
                r0 = ((n_sub - 1 - j) if rev else j) * c
                if not isinstance(j, int):
                    r0 = pl.multiple_of(r0, c)
                for h in range(HG_HEADS):
                    cs = slice(h * HG_HEAD_DIM, (h + 1) * HG_HEAD_DIM)
                    ld = lambda ref: ref[0, pl.ds(r0, c), cs]
                    q, kk, bb = ld(q_ref), ld(k_ref), b_ref[di, pl.ds(r0, c), cs]
                    scores, growth, decay = _hgrn_scores(q, kk, bb, lev[rev], rev, base)
                    work.append((di, rev, h, r0, cs, q, kk, ld(v_ref), bb, scores, growth, decay, o_ref))
            for di, rev, h, r0, cs, q, kk, v, bb, scores, growth, decay, o_ref in work:
                o, st = _hgrn_output(scores, q, kk, v, bb, st_ref[di, h], rev, growth, decay)
                st_ref[di, h] = st
                o_ref[0, pl.ds(r0, c), cs] = o.astype(o_ref.dtype)
            return carry

        if base == 1:
            lax.fori_loop(0, n_sub, body, 0)
        else:
            for j in range(n_sub):
                body(j, 0)

    pl.when(fast)(lambda: run(HG_BASE))
    pl.when(jnp.logical_not(fast))(lambda: run(1))


def _hgrn(hq, hv, lff, kf, lfb, kb):
    b, s, w = hq.shape
    ns = s // HG_STEP
    fwd = pl.BlockSpec((1, HG_STEP, w), lambda i, j: (i, j, 0))
    bwd = pl.BlockSpec((1, HG_STEP, w), lambda i, j: (i, ns - 1 - j, 0))
    sds = jax.ShapeDtypeStruct((b, s, w), BF16)
    return pl.pallas_call(
        _hgrn_kernel,
        grid=(b, ns),
        in_specs=[fwd, fwd, fwd, fwd, bwd, bwd, bwd, bwd],
        out_specs=[fwd, bwd],
        out_shape=[sds, sds],
        scratch_shapes=[pltpu.VMEM((2, HG_HEADS, HG_HEAD_DIM, HG_HEAD_DIM), F32),
                        pltpu.VMEM((2, HG_STEP, w), F32)],
        compiler_params=pltpu.CompilerParams(
            dimension_semantics=("parallel", "arbitrary"), vmem_limit_bytes=V7X_VMEM_LIMIT),
        name="hgrn",
    )(hq, hv, lff, kf, hq, hv, lfb, kb)


def _rms(x, g):
    ms = jnp.mean(x * x, axis=-1, keepdims=True)
    return x * lax.rsqrt(ms + NORM_EPS) * g


def _out_kernel(a_ref, of_ref, ob_ref, hg_ref, x_ref, p_ref,
                gh_ref, wo_ref, gp_ref, wg_ref, wp_ref, gf_ref, o_ref):
    sub = x_ref.shape[0] // OUT_SPLIT
    tiles = [slice(u * sub, (u + 1) * sub) for u in range(OUT_SPLIT)]
    gh = gh_ref[...]

    def mixed(rows):
        r = of_ref[rows, :].astype(F32) + ob_ref[rows, :].astype(F32)
        parts = [a_ref[rows, :]]
        for h in range(HG_HEADS):
            cs = slice(h * HG_HEAD_DIM, (h + 1) * HG_HEAD_DIM)
            y = _rms(r[:, cs], gh[:, cs]) * hg_ref[rows, cs].astype(F32)
            parts.append(y.astype(BF16))
        return jnp.concatenate(parts, axis=-1)

    cats = [mixed(rows) for rows in tiles]
    pps = [jnp.dot(p_ref[rows, :].astype(BF16), wp_ref[...], preferred_element_type=F32)
           for rows in tiles]
    h1s = [x_ref[rows, :] + jnp.dot(cat, wo_ref[...], preferred_element_type=F32)
           for rows, cat in zip(tiles, cats)]
    gates = [_sigmoid(jnp.dot(_rms(h1, gp_ref[...]).astype(BF16), wg_ref[...],
                              preferred_element_type=F32)) for h1 in h1s]
    for rows, h1, gate, pp in zip(tiles, h1s, gates, pps):
        o_ref[rows, :] = _rms(h1 + gate * pp, gf_ref[...])


def _out(a, o_f, o_b, hg, x2, p2, hgrn_norm, w_out, ln_ple, w_pg, w_pp, ln_final):
    n = x2.shape[0]
    row = lambda w: pl.BlockSpec((OUT_TM, w), lambda i: (i, 0))
    full = lambda r, c: pl.BlockSpec((r, c), lambda i: (0, 0))
    return pl.pallas_call(
        _out_kernel,
        grid=(n // OUT_TM,),
        in_specs=[row(NA_WIDTH), row(HG_WIDTH), row(HG_WIDTH), row(HG_WIDTH),
                  row(D_MODEL), row(PLE_DIM),
                  full(1, HG_WIDTH), full(D_MODEL, D_MODEL), full(1, D_MODEL),
                  full(D_MODEL, D_MODEL), full(PLE_DIM, D_MODEL), full(1, D_MODEL)],
        out_specs=row(D_MODEL),
        out_shape=jax.ShapeDtypeStruct((n, D_MODEL), F32),
        compiler_params=pltpu.CompilerParams(
            dimension_semantics=("parallel",), vmem_limit_bytes=V7X_VMEM_LIMIT),
        name="out",
    )(a, o_f, o_b, hg, x2, p2, hgrn_norm, w_out, ln_ple, w_pg, w_pp, ln_final)


def kernel(x, p, ln_mix, w_in, rpb, lb_logits, attn_norm, hgrn_norm, w_out, ln_ple, w_pg, w_pp, ln_final):
    depth = p.shape[0]
    assert depth == 1 and lb_logits.shape[0] == 2
    b, s, d = x.shape
    n = b * s
    rows = s // GRID_W
    assert d == D_MODEL and n % PROJ_TM == 0 and n % OUT_TM == 0 and s % HG_STEP == 0
    assert rows % min(NA_ROWS_PER_ITER, rows) == 0 and rows >= NA_ROWS

    x2 = x.reshape(n, d)
    (aq, ak, av, ag, hq, lff, kf, lfb, kb, hv, hg) = _proj(
        x2, ln_mix[0:1], w_in[0].astype(BF16), lb_logits)

    r3 = lambda t: t.reshape(b, s, t.shape[-1])
    a = _attn(rpb[0].reshape(-1), r3(aq), r3(ak), r3(av), r3(ag), attn_norm[0:1])
    o_f, o_b = _hgrn(r3(hq), r3(hv), r3(lff), r3(kf), r3(lfb), r3(kb))

    out = _out(a.reshape(n, NA_WIDTH), o_f.reshape(n, HG_WIDTH), o_b.reshape(n, HG_WIDTH), hg,
               x2, p[0].reshape(n, PLE_DIM), hgrn_norm[0:1], w_out[0].astype(BF16),
               ln_ple[0:1], w_pg[0].astype(BF16), w_pp[0].astype(BF16), ln_final.reshape(1, d))
    return out.reshape(b, s, d)
```

```python
import jax
import jax.numpy as jnp
from jax import lax
from jax.experimental import pallas as pl
from jax.experimental.pallas import tpu as pltpu

F32 = jnp.float32
BF16 = jnp.bfloat16

D_MODEL = 1024
GRID_W = 64
NA_WIDTH = 512
HG_WIDTH = 512
NA_HEADS = 8
NA_HEAD_DIM = 64
NA_ROWS = 8
NA_COLS = 16
HG_HEAD_DIM = 128
HG_HEADS = 4
PLE_DIM = 256
NORM_EPS = 1e-6
N_GROUPS = 9
GROUP_WIDTH = NA_WIDTH
assert HG_WIDTH == GROUP_WIDTH
PROJ_TOTAL = N_GROUPS * GROUP_WIDTH

V7X_LANES = 128
V7X_VMEM_LIMIT = 56 * 1024 * 1024
MASK_VALUE = -1e30
LOG2E = 1.4426950408889634

PROJ_TM = 1024
PROJ_SPLIT = 8
OUT_TM = 1024
OUT_SPLIT = 2
OUT_IN_BUFFERS = 3
HG_CHUNK = 128
HG_STEP = 2048
HG_BASE = HG_CHUNK
HG_MAX_LOG2_RANGE = 110.0
NA_PAIRS = NA_HEADS // 2
NA_ROWS_PER_ITER = 64


def _sigmoid(x):
    return 1.0 / (1.0 + jnp.exp(-x))


def _proj_kernel(x_ref, g_ref, w_ref, lbl_ref,
                 aq_ref, ak_ref, av_ref, ag_ref, hq_ref,
                 lff_ref, kf_ref, lfb_ref, kb_ref, hv_ref, hg_ref):
    l0 = lbl_ref[0]
    l1 = lbl_ref[1]
    m = jnp.maximum(l0, l1)
    e0 = jnp.exp(l0 - m)
    e1 = jnp.exp(l1 - m)
    lb = e0 / (e0 + e1)

    sub = x_ref.shape[0] // PROJ_SPLIT
    tiles = [slice(u * sub, (u + 1) * sub) for u in range(PROJ_SPLIT)]
    hns = []
    for rows in tiles:
        x = x_ref[rows, :]
        ms = jnp.mean(x * x, axis=-1, keepdims=True)
        hns.append((x * lax.rsqrt(ms + NORM_EPS) * g_ref[...]).astype(BF16))

    def silu(t):
        return t * _sigmoid(t)

    def forget(lbj):
        def act(t):
            f = lbj + (1.0 - lbj) * _sigmoid(t)
            return jnp.log(f), 1.0 - f
        return act

    scale_q = lambda t: t * (NA_HEAD_DIM ** -0.5 * LOG2E)
    ident = lambda t: t
    groups = ((scale_q, (aq_ref,)), (ident, (ak_ref,)), (ident, (av_ref,)), (silu, (ag_ref,)),
              (silu, (hq_ref,)), (forget(lb[0:1, :]), (lff_ref, kf_ref)),
              (forget(lb[1:2, :]), (lfb_ref, kb_ref)), (ident, (hv_ref,)), (silu, (hg_ref,)))
    for j, (act, out_refs) in enumerate(groups):
        w = w_ref[:, j * GROUP_WIDTH:(j + 1) * GROUP_WIDTH]
        for rows, hn in zip(tiles, hns):
            res = act(jnp.dot(hn, w, preferred_element_type=F32))
            for ref, val in zip(out_refs, res if isinstance(res, tuple) else (res,)):
                ref[rows, :] = val.astype(BF16)


def _proj(x2, ln_mix, w_in, lb_logits):
    n = x2.shape[0]
    out_sds = jax.ShapeDtypeStruct((n, GROUP_WIDTH), BF16)
    row_spec = pl.BlockSpec((PROJ_TM, GROUP_WIDTH), lambda i: (i, 0))
    return pl.pallas_call(
        _proj_kernel,
        grid=(n // PROJ_TM,),
        in_specs=[
            pl.BlockSpec((PROJ_TM, D_MODEL), lambda i: (i, 0)),
            pl.BlockSpec((1, D_MODEL), lambda i: (0, 0)),
            pl.BlockSpec((D_MODEL, PROJ_TOTAL), lambda i: (0, 0), pipeline_mode=pl.Buffered(1)),
            pl.BlockSpec((2, 2, HG_WIDTH), lambda i: (0, 0, 0)),
        ],
        out_specs=[row_spec] * 11,
        out_shape=[out_sds] * 11,
        compiler_params=pltpu.CompilerParams(
            dimension_semantics=("parallel",), vmem_limit_bytes=V7X_VMEM_LIMIT),
        name="proj",
    )(x2, ln_mix, w_in, lb_logits)


def _attn_build_table(rpb_ref, tab_ref, pair):
    kh = tab_ref.shape[0]
    n_dr, n_dc = 2 * NA_ROWS - 1, 2 * NA_COLS - 1
    c_idx = lax.broadcasted_iota(jnp.int32, (GRID_W, V7X_LANES), 0)
    kc_idx = lax.broadcasted_iota(jnp.int32, (GRID_W, V7X_LANES), 1) % GRID_W
    start = jnp.clip(c_idx - NA_COLS // 2, 0, GRID_W - NA_COLS)
    valid = (kc_idx >= start) & (kc_idx < start + NA_COLS)
    diff = kc_idx - c_idx + NA_COLS - 1
    low_half = lax.broadcasted_iota(jnp.int32, (GRID_W, V7X_LANES), 1) < GRID_W
    for j in range(2):
        head = 2 * pair + j
        toeplitz = {}
        for dr in range(NA_ROWS - kh, NA_ROWS - 1 + kh):
            t = jnp.full((GRID_W, V7X_LANES), MASK_VALUE, F32)
            for dc in range(n_dc):
                t = jnp.where(diff == dc, rpb_ref[(head * n_dr + dr) * n_dc + dc] * LOG2E, t)
            toeplitz[dr] = jnp.where(valid, t, MASK_VALUE)
        for d in range(kh):
            for i in range(0, kh, 2):
                blk = jnp.where(low_half, toeplitz[i - d + NA_ROWS - 1], toeplitz[i + 1 - d + NA_ROWS - 1])
                tab_ref[d, j * GRID_W:(j + 1) * GRID_W, i * GRID_W:(i + 2) * GRID_W] = blk


def _attn_kernel(rpb_ref, q_ref, k_ref, v_ref, sg_ref, g_ref, o_ref, tab_ref):
    rows = q_ref.shape[1] // GRID_W
    kh = min(NA_ROWS, rows)
    nkeys = kh * GRID_W
    lane = lax.broadcasted_iota(jnp.int32, (GRID_W, V7X_LANES), 1)
    first = lane < NA_HEAD_DIM
    m0 = first.astype(F32).astype(BF16)
    m1 = (1.0 - first.astype(F32)).astype(BF16)
    gain = g_ref[...]

    @pl.when(pl.program_id(1) == 0)
    def _():
        _attn_build_table(rpb_ref, tab_ref, pl.program_id(0))

    def scores(r):
        rs = jnp.clip(r - kh // 2, 0, rows - kh)
        q0 = pl.multiple_of(r * GRID_W, GRID_W)
        k0 = pl.multiple_of(rs * GRID_W, GRID_W)
        q = q_ref[0, pl.ds(q0, GRID_W), :]
        kk = k_ref[0, pl.ds(k0, nkeys), :]
        qm = jnp.concatenate([m0 * q, m1 * q], axis=0)
        s = lax.dot_general(qm, kk, (((1,), (1,)), ((), ())), preferred_element_type=F32)
        return q0, k0, s + tab_ref[r - rs]

    def softmax(s):
        return jnp.exp2(s - jnp.max(s, axis=-1, keepdims=True)).astype(BF16)

    ones = jnp.ones((nkeys, V7X_LANES), BF16)

    def weighted_values(k0, p):
        vals = jnp.concatenate([v_ref[0, pl.ds(k0, nkeys), :], ones], axis=1)
        return jnp.dot(p, vals, preferred_element_type=F32)

    def finish(q0, o2):
        o2 = o2[:, :V7X_LANES] / o2[:, V7X_LANES:]
        o = jnp.where(first, o2[:GRID_W], o2[GRID_W:])
        sq = o * o
        ms0 = jnp.sum(jnp.where(first, sq, 0.0), axis=-1, keepdims=True)
        ms1 = jnp.sum(jnp.where(first, 0.0, sq), axis=-1, keepdims=True)
        ms = jnp.where(first, ms0, ms1) * (1.0 / NA_HEAD_DIM)
        y = o * lax.rsqrt(ms + NORM_EPS) * gain
        y = y * sg_ref[0, pl.ds(q0, GRID_W), :].astype(F32)
        o_ref[0, pl.ds(q0, GRID_W), :] = y.astype(BF16)

    per_iter = min(NA_ROWS_PER_ITER, rows)

    def body(i, carry):
        rs_ = [scores(i * per_iter + u) for u in range(per_iter)]
        ps = [softmax(s) for _, _, s in rs_]
        os_ = [weighted_values(k0, p) for (_, k0, _), p in zip(rs_, ps)]
        for (q0, _, _), o2 in zip(rs_, os_):
            finish(q0, o2)
        return carry

    lax.fori_loop(0, rows // per_iter, body, 0)


def _attn(rpb_flat, aq, ak, av, ag, attn_norm):
    b, s, _ = aq.shape
    kh = min(NA_ROWS, s // GRID_W)
    assert kh % 2 == 0
    blk = pl.BlockSpec((1, s, V7X_LANES), lambda hp, i: (i, 0, hp))
    return pl.pallas_call(
        _attn_kernel,
        grid=(NA_PAIRS, b),
        in_specs=[pl.BlockSpec(memory_space=pltpu.SMEM), blk, blk, blk, blk,
                  pl.BlockSpec((1, V7X_LANES), lambda hp, i: (0, hp))],
        out_specs=blk,
        out_shape=jax.ShapeDtypeStruct((b, s, NA_WIDTH), BF16),
        scratch_shapes=[pltpu.VMEM((kh, 2 * GRID_W, kh * GRID_W), F32)],
        compiler_params=pltpu.CompilerParams(
            dimension_semantics=("parallel", "arbitrary"), vmem_limit_bytes=V7X_VMEM_LIMIT),
        name="attn",
    )(rpb_flat, aq, ak, av, ag, attn_norm)


def _block_ref(b, n, rev):
    c, w = b.shape
    pos = n if rev else n - 1
    if 2 * n >= 8:
        b3 = b.reshape(c // (2 * n), 2 * n, w)
        return jnp.broadcast_to(b3[:, pos:pos + 1, :], b3.shape).reshape(c, w)
    row = lax.broadcasted_iota(jnp.int32, b.shape, 0)
    if n == 1:
        if rev:
            return jnp.where(row % 2 == 0, pltpu.roll(b, c - 1, 0), b)
        return jnp.where(row % 2 == 1, pltpu.roll(b, 1, 0), b)
    assert n == 2
    if rev:
        c1 = jnp.where(row % 2 == 1, pltpu.roll(b, 1, 0), b)
        return jnp.where(row % 4 < 2, pltpu.roll(c1, c - 2, 0), c1)
    c1 = jnp.where(row % 2 == 0, pltpu.roll(b, c - 1, 0), b)
    return jnp.where(row % 4 >= 2, pltpu.roll(c1, 2, 0), c1)


def _level_decay(b, n, rev):
    c, w = b.shape
    if n < 8:
        return jnp.exp2(-jnp.abs(b - _block_ref(b, n, rev)))
    b3 = b.reshape(c // (2 * n), 2 * n, w)
    lo, hi = b3[:, :n], b3[:, n:]
    if rev:
        ref = b3[:, n:n + 1]
        parts = [jnp.exp2(lo - ref), jnp.exp2(ref - hi)]
    else:
        ref = b3[:, n - 1:n]
        parts = [jnp.exp2(ref - lo), jnp.exp2(hi - ref)]
    return jnp.concatenate(parts, axis=1).reshape(c, w)


def _base_ref(b3, rev):
    base = b3.shape[1]
    pos = base // 2 if rev else base // 2 - 1
    return b3[:, pos:pos + 1]


def _level_ids(c, rev, base):
    t_idx = lax.broadcasted_iota(jnp.int32, (c, c), 0)
    s_idx = lax.broadcasted_iota(jnp.int32, (c, c), 1)
    x = t_idx ^ s_idx
    lev = jnp.ones((c, c), jnp.int32)
    n = 2 * base
    while n < c:
        lev = lev + (x >= n).astype(jnp.int32)
        n *= 2
    before = (t_idx < s_idx) if rev else (t_idx > s_idx)
    lev = jnp.where(before & (x >= base), lev, -1)
    return jnp.where((x < base) & (before | (t_idx == s_idx)), 0, lev)


def _hgrn_scores(q, kk, b, lev, rev, base):
    c, w = q.shape
    nt = (((1,), (1,)), ((), ()))

    if base == 1:
        scores, growth, decay = lax.dot_general(q, kk, nt, preferred_element_type=F32), None, None
    else:
        b3 = b.reshape(c // base, base, w)
        d = (b3 - _base_ref(b3, rev)).reshape(c, w)
        growth, decay = jnp.exp2(-d).astype(BF16), jnp.exp2(d).astype(BF16)
        scores = lax.dot_general(decay * q, growth * kk, nt, preferred_element_type=F32)
    scores = jnp.where(lev == 0, scores, 0.0)
    n, li = base, 1
    while n < c:
        e = _level_decay(b, n, rev).astype(BF16)
        sl = lax.dot_general(e * q, e * kk, nt, preferred_element_type=F32)
        scores = jnp.where(lev == li, sl, scores)
        n, li = 2 * n, li + 1
    return scores.astype(BF16), growth, decay


def _hgrn_output(scores, q, kk, v, b, st, rev, growth, decay):
    c, w = q.shape
    btot = b[0:1, :] if rev else b[c - 1:c, :]
    if decay is None:
        from_start, to_end = jnp.exp2(b).astype(BF16), jnp.exp2(btot - b).astype(BF16)
    else:
        bmid = _base_ref(b.reshape(1, c, w), rev).reshape(1, w)
        from_start = decay * jnp.exp2(bmid).astype(BF16)
        to_end = growth * jnp.exp2(btot - bmid).astype(BF16)
    lhs = jnp.concatenate([scores, from_start * q], axis=1)
    rhs = jnp.concatenate([v, st.astype(BF16)], axis=0)
    o = jnp.dot(lhs, rhs, preferred_element_type=F32)

    kd = to_end * kk
    upd = lax.dot_general(kd, v, (((0,), (0,)), ((), ())), preferred_element_type=F32)
    keep = jnp.broadcast_to(jnp.exp2(btot), (w, w)).T
    return o, st * keep + upd


def _hgrn_kernel(qf_ref, vf_ref, lff_ref, kf_ref, qb_ref, vb_ref, lfb_ref, kb_ref,
                 of_ref, ob_ref, st_ref, b_ref):
    @pl.when(pl.program_id(1) == 0)
    def _():
        st_ref[...] = jnp.zeros_like(st_ref)

    c = HG_CHUNK
    n_sub = qf_ref.shape[1] // c
    t_idx = lax.broadcasted_iota(jnp.int32, (c, c), 0)
    u_idx = lax.broadcasted_iota(jnp.int32, (c, c), 1)
    dirs = ((0, False, qf_ref, kf_ref, vf_ref, lff_ref, of_ref),
            (1, True, qb_ref, kb_ref, vb_ref, lfb_ref, ob_ref))

    widest = jnp.zeros((c // HG_BASE, 1, lff_ref.shape[2]), F32)
    for di, rev, _, _, _, lf_ref, _ in dirs:
        tri = ((u_idx >= t_idx) if rev else (u_idx <= t_idx)).astype(F32).astype(BF16)
        for j in range(n_sub):
            rows = slice(j * c, (j + 1) * c)
            bj = jnp.dot(tri, lf_ref[0, rows, :], preferred_element_type=F32) * LOG2E
            b_ref[di, rows, :] = bj
            b3 = bj.reshape(c // HG_BASE, HG_BASE, bj.shape[1])
            ref = _base_ref(b3, rev)
            widest = jnp.maximum(widest, jnp.maximum(jnp.abs(b3[:, 0:1] - ref),
                                                     jnp.abs(b3[:, HG_BASE - 1:HG_BASE] - ref)))
    fast = jnp.max(widest) <= HG_MAX_LOG2_RANGE

    def run(base):
        lev = {rev: _level_ids(c, rev, base) for rev in (False, True)}

        def body(j, carry):
            work = []
            for di, rev, q_ref, k_ref, v_ref, _, o_ref in dirs:
                r0 = ((n_sub - 1 - j) if rev else j) * c
                if not isinstance(j, int):
                    r0 = pl.multiple_of(r0, c)
                for h in range(HG_HEADS):
                    cs = slice(h * HG_HEAD_DIM, (h + 1) * HG_HEAD_DIM)
                    ld = lambda ref: ref[0, pl.ds(r0, c), cs]
                    q, kk, bb = ld(q_ref), ld(k_ref), b_ref[di, pl.ds(r0, c), cs]
                    scores, growth, decay = _hgrn_scores(q, kk, bb, lev[rev], rev, base)
                    work.append((di, rev, h, r0, cs, q, kk, ld(v_ref), bb, scores, growth, decay, o_ref))
            for di, rev, h, r0, cs, q, kk, v, bb, scores, growth, decay, o_ref in work:
                o, st = _hgrn_output(scores, q, kk, v, bb, st_ref[di, h], rev, growth, decay)
                st_ref[di, h] = st
                o_ref[0, pl.ds(r0, c), cs] = o.astype(o_ref.dtype)
            return carry

        if base == 1:
            lax.fori_loop(0, n_sub, body, 0)
        else:
            for j in range(n_sub):
                body(j, 0)

    pl.when(fast)(lambda: run(HG_BASE))
    pl.when(jnp.logical_not(fast))(lambda: run(1))


def _hgrn(hq, hv, lff, kf, lfb, kb):
    b, s, w = hq.shape
    ns = s // HG_STEP
    fwd = pl.BlockSpec((1, HG_STEP, w), lambda i, j: (i, j, 0))
    bwd = pl.BlockSpec((1, HG_STEP, w), lambda i, j: (i, ns - 1 - j, 0))
    sds = jax.ShapeDtypeStruct((b, s, w), BF16)
    return pl.pallas_call(
        _hgrn_kernel,
        grid=(b, ns),
        in_specs=[fwd, fwd, fwd, fwd, bwd, bwd, bwd, bwd],
        out_specs=[fwd, bwd],
        out_shape=[sds, sds],
        scratch_shapes=[pltpu.VMEM((2, HG_HEADS, HG_HEAD_DIM, HG_HEAD_DIM), F32),
                        pltpu.VMEM((2, HG_STEP, w), F32)],
        compiler_params=pltpu.CompilerParams(
            dimension_semantics=("parallel", "arbitrary"), vmem_limit_bytes=V7X_VMEM_LIMIT),
        name="hgrn",
    )(hq, hv, lff, kf, hq, hv, lfb, kb)


def _rms(x, g):
    ms = jnp.mean(x * x, axis=-1, keepdims=True)
    return x * lax.rsqrt(ms + NORM_EPS) * g


def _out_kernel(a_ref, of_ref, ob_ref, hg_ref, x_ref, p_ref,
                gh_ref, wo_ref, gp_ref, wg_ref, wp_ref, gf_ref, o_ref):
    sub = x_ref.shape[0] // OUT_SPLIT
    tiles = [slice(u * sub, (u + 1) * sub) for u in range(OUT_SPLIT)]
    gh = gh_ref[...]

    def mixed(rows):
        r = of_ref[rows, :].astype(F32) + ob_ref[rows, :].astype(F32)
        parts = [a_ref[rows, :]]
        for h in range(HG_HEADS):
            cs = slice(h * HG_HEAD_DIM, (h + 1) * HG_HEAD_DIM)
            y = _rms(r[:, cs], gh[:, cs]) * hg_ref[rows, cs].astype(F32)
            parts.append(y.astype(BF16))
        return jnp.concatenate(parts, axis=-1)

    cats = [mixed(rows) for rows in tiles]
    pps = [jnp.dot(p_ref[rows, :].astype(BF16), wp_ref[...], preferred_element_type=F32)
           for rows in tiles]
    h1s = [x_ref[rows, :] + jnp.dot(cat, wo_ref[...], preferred_element_type=F32)
           for rows, cat in zip(tiles, cats)]
    gates = [_sigmoid(jnp.dot(_rms(h1, gp_ref[...]).astype(BF16), wg_ref[...],
                              preferred_element_type=F32)) for h1 in h1s]
    for rows, h1, gate, pp in zip(tiles, h1s, gates, pps):
        o_ref[rows, :] = _rms(h1 + gate * pp, gf_ref[...])


def _out(a, o_f, o_b, hg, x2, p2, hgrn_norm, w_out, ln_ple, w_pg, w_pp, ln_final):
    n = x2.shape[0]

    def outer(a_hbm, of_hbm, ob_hbm, hg_hbm, x_hbm, p_hbm,
              gh_ref, wo_ref, gp_ref, wg_ref, wp_ref, gf_ref, o_hbm):
        def inner(a_ref, of_ref, ob_ref, hg_ref, x_ref, p_ref, o_ref):
            _out_kernel(a_ref, of_ref, ob_ref, hg_ref, x_ref, p_ref,
                        gh_ref, wo_ref, gp_ref, wg_ref, wp_ref, gf_ref, o_ref)

        def row(w, buffers):
            return pl.BlockSpec((OUT_TM, w), lambda i: (i, 0), pipeline_mode=pl.Buffered(buffers))

        pltpu.emit_pipeline(
            inner, grid=(n // OUT_TM,),
            in_specs=[row(NA_WIDTH, OUT_IN_BUFFERS), row(HG_WIDTH, OUT_IN_BUFFERS),
                      row(HG_WIDTH, OUT_IN_BUFFERS), row(HG_WIDTH, OUT_IN_BUFFERS),
                      row(D_MODEL, OUT_IN_BUFFERS), row(PLE_DIM, OUT_IN_BUFFERS)],
            out_specs=[row(D_MODEL, 2)],
        )(a_hbm, of_hbm, ob_hbm, hg_hbm, x_hbm, p_hbm, o_hbm)

    hbm = pl.BlockSpec(memory_space=pl.ANY)
    vmem = pl.BlockSpec(memory_space=pltpu.VMEM)
    return pl.pallas_call(
        outer,
        in_specs=[hbm] * 6 + [vmem] * 6,
        out_specs=hbm,
        out_shape=jax.ShapeDtypeStruct((n, D_MODEL), F32),
        compiler_params=pltpu.CompilerParams(vmem_limit_bytes=V7X_VMEM_LIMIT),
        name="out",
    )(a, o_f, o_b, hg, x2, p2, hgrn_norm, w_out, ln_ple, w_pg, w_pp, ln_final)


def kernel(x, p, ln_mix, w_in, rpb, lb_logits, attn_norm, hgrn_norm, w_out, ln_ple, w_pg, w_pp, ln_final):
    depth = p.shape[0]
    assert depth == 1 and lb_logits.shape[0] == 2
    b, s, d = x.shape
    n = b * s
    rows = s // GRID_W
    assert d == D_MODEL and n % PROJ_TM == 0 and n % OUT_TM == 0 and s % HG_STEP == 0
    assert rows % min(NA_ROWS_PER_ITER, rows) == 0 and rows >= NA_ROWS

    x2 = x.reshape(n, d)
    (aq, ak, av, ag, hq, lff, kf, lfb, kb, hv, hg) = _proj(
        x2, ln_mix[0:1], w_in[0].astype(BF16), lb_logits)

    r3 = lambda t: t.reshape(b, s, t.shape[-1])
    a = _attn(rpb[0].reshape(-1), r3(aq), r3(ak), r3(av), r3(ag), attn_norm[0:1])
    o_f, o_b = _hgrn(r3(hq), r3(hv), r3(lff), r3(kf), r3(lfb), r3(kb))

    out = _out(a.reshape(n, NA_WIDTH), o_f.reshape(n, HG_WIDTH), o_b.reshape(n, HG_WIDTH), hg,
               x2, p[0].reshape(n, PLE_DIM), hgrn_norm[0:1], w_out[0].astype(BF16),
               ln_ple[0:1], w_pg[0].astype(BF16), w_pp[0].astype(BF16), ln_final.reshape(1, d))
    return out.reshape(b, s, d)
```

```python
import jax
import jax.numpy as jnp
from jax import lax
from jax.experimental import pallas as pl
from jax.experimental.pallas import tpu as pltpu

F32 = jnp.float32
BF16 = jnp.bfloat16

D_MODEL = 1024
GRID_W = 64
NA_WIDTH = 512
HG_WIDTH = 512
NA_HEADS = 8
NA_HEAD_DIM = 64
NA_ROWS = 8
NA_COLS = 16
HG_HEAD_DIM = 128
HG_HEADS = 4
PLE_DIM = 256
NORM_EPS = 1e-6
N_GROUPS = 9
GROUP_WIDTH = NA_WIDTH
assert HG_WIDTH == GROUP_WIDTH
PROJ_TOTAL = N_GROUPS * GROUP_WIDTH

V7X_LANES = 128
V7X_VMEM_LIMIT = 56 * 1024 * 1024
MASK_VALUE = -1e30
LOG2E = 1.4426950408889634

PROJ_TM = 1024
PROJ_SPLIT = 8
OUT_TM = 1024
OUT_SPLIT = 2
HG_CHUNK = 128
HG_STEP = 2048
HG_BASE = HG_CHUNK
HG_MAX_LOG2_RANGE = 110.0
NA_PAIRS = NA_HEADS // 2
NA_ROWS_PER_ITER = 64


def _sigmoid(x):
    return 1.0 / (1.0 + jnp.exp(-x))


def _proj_kernel(x_ref, g_ref, w_ref, lbl_ref,
                 aq_ref, ak_ref, av_ref, ag_ref, hq_ref,
                 lff_ref, kf_ref, lfb_ref, kb_ref, hv_ref, hg_ref):
    l0 = lbl_ref[0]
    l1 = lbl_ref[1]
    m = jnp.maximum(l0, l1)
    e0 = jnp.exp(l0 - m)
    e1 = jnp.exp(l1 - m)
    lb = e0 / (e0 + e1)

    sub = x_ref.shape[0] // PROJ_SPLIT
    tiles = [slice(u * sub, (u + 1) * sub) for u in range(PROJ_SPLIT)]
    hns = []
    for rows in tiles:
        x = x_ref[rows, :]
        ms = jnp.mean(x * x, axis=-1, keepdims=True)
        hns.append((x * lax.rsqrt(ms + NORM_EPS) * g_ref[...]).astype(BF16))

    def silu(t):
        return t * _sigmoid(t)

    def forget(lbj):
        def act(t):
            f = lbj + (1.0 - lbj) * _sigmoid(t)
            return jnp.log(f), 1.0 - f
        return act

    scale_q = lambda t: t * (NA_HEAD_DIM ** -0.5 * LOG2E)
    ident = lambda t: t
    groups = ((scale_q, (aq_ref,)), (ident, (ak_ref,)), (ident, (av_ref,)), (silu, (ag_ref,)),
              (silu, (hq_ref,)), (forget(lb[0:1, :]), (lff_ref, kf_ref)),
              (forget(lb[1:2, :]), (lfb_ref, kb_ref)), (ident, (hv_ref,)), (silu, (hg_ref,)))
    for j, (act, out_refs) in enumerate(groups):
        w = w_ref[:, j * GROUP_WIDTH:(j + 1) * GROUP_WIDTH]
        for rows, hn in zip(tiles, hns):
            res = act(jnp.dot(hn, w, preferred_element_type=F32))
            for ref, val in zip(out_refs, res if isinstance(res, tuple) else (res,)):
                ref[rows, :] = val.astype(BF16)


def _proj(x2, ln_mix, w_in, lb_logits):
    n = x2.shape[0]
    out_sds = jax.ShapeDtypeStruct((n, GROUP_WIDTH), BF16)
    row_spec = pl.BlockSpec((PROJ_TM, GROUP_WIDTH), lambda i: (i, 0))
    return pl.pallas_call(
        _proj_kernel,
        grid=(n // PROJ_TM,),
        in_specs=[
            pl.BlockSpec((PROJ_TM, D_MODEL), lambda i: (i, 0)),
            pl.BlockSpec((1, D_MODEL), lambda i: (0, 0)),
            pl.BlockSpec((D_MODEL, PROJ_TOTAL), lambda i: (0, 0), pipeline_mode=pl.Buffered(1)),
            pl.BlockSpec((2, 2, HG_WIDTH), lambda i: (0, 0, 0)),
        ],
        out_specs=[row_spec] * 11,
        out_shape=[out_sds] * 11,
        compiler_params=pltpu.CompilerParams(
            dimension_semantics=("parallel",), vmem_limit_bytes=V7X_VMEM_LIMIT),
        name="proj",
    )(x2, ln_mix, w_in, lb_logits)


def _attn_build_table(rpb_ref, tab_ref, pair):
    kh = tab_ref.shape[0]
    n_dr, n_dc = 2 * NA_ROWS - 1, 2 * NA_COLS - 1
    c_idx = lax.broadcasted_iota(jnp.int32, (GRID_W, V7X_LANES), 0)
    kc_idx = lax.broadcasted_iota(jnp.int32, (GRID_W, V7X_LANES), 1) % GRID_W
    start = jnp.clip(c_idx - NA_COLS // 2, 0, GRID_W - NA_COLS)
    valid = (kc_idx >= start) & (kc_idx < start + NA_COLS)
    diff = kc_idx - c_idx + NA_COLS - 1
    low_half = lax.broadcasted_iota(jnp.int32, (GRID_W, V7X_LANES), 1) < GRID_W
    for j in range(2):
        head = 2 * pair + j
        toeplitz = {}
        for dr in range(NA_ROWS - kh, NA_ROWS - 1 + kh):
            t = jnp.full((GRID_W, V7X_LANES), MASK_VALUE, F32)
            for dc in range(n_dc):
                t = jnp.where(diff == dc, rpb_ref[(head * n_dr + dr) * n_dc + dc] * LOG2E, t)
            toeplitz[dr] = jnp.where(valid, t, MASK_VALUE)
        for d in range(kh):
            for i in range(0, kh, 2):
                blk = jnp.where(low_half, toeplitz[i - d + NA_ROWS - 1], toeplitz[i + 1 - d + NA_ROWS - 1])
                tab_ref[d, j * GRID_W:(j + 1) * GRID_W, i * GRID_W:(i + 2) * GRID_W] = blk


def _attn_kernel(rpb_ref, q_ref, k_ref, v_ref, sg_ref, g_ref, o_ref, tab_ref):
    rows = q_ref.shape[1] // GRID_W
    kh = min(NA_ROWS, rows)
    nkeys = kh * GRID_W
    lane = lax.broadcasted_iota(jnp.int32, (GRID_W, V7X_LANES), 1)
    first = lane < NA_HEAD_DIM
    m0 = first.astype(F32).astype(BF16)
    m1 = (1.0 - first.astype(F32)).astype(BF16)
    gain = g_ref[...]

    @pl.when(pl.program_id(1) == 0)
    def _():
        _attn_build_table(rpb_ref, tab_ref, pl.program_id(0))

    def scores(r):
        rs = jnp.clip(r - kh // 2, 0, rows - kh)
        q0 = pl.multiple_of(r * GRID_W, GRID_W)
        k0 = pl.multiple_of(rs * GRID_W, GRID_W)
        q = q_ref[0, pl.ds(q0, GRID_W), :]
        kk = k_ref[0, pl.ds(k0, nkeys), :]
        qm = jnp.concatenate([m0 * q, m1 * q], axis=0)
        s = lax.dot_general(qm, kk, (((1,), (1,)), ((), ())), preferred_element_type=F32)
        return q0, k0, s + tab_ref[r - rs]

    def softmax(s):
        return jnp.exp2(s - jnp.max(s, axis=-1, keepdims=True)).astype(BF16)

    ones = jnp.ones((nkeys, V7X_LANES), BF16)

    def weighted_values(k0, p):
        vals = jnp.concatenate([v_ref[0, pl.ds(k0, nkeys), :], ones], axis=1)
        return jnp.dot(p, vals, preferred_element_type=F32)

    def finish(q0, o2):
        o2 = o2[:, :V7X_LANES] / o2[:, V7X_LANES:]
        o = jnp.where(first, o2[:GRID_W], o2[GRID_W:])
        sq = o * o
        ms0 = jnp.sum(jnp.where(first, sq, 0.0), axis=-1, keepdims=True)
        ms1 = jnp.sum(jnp.where(first, 0.0, sq), axis=-1, keepdims=True)
        ms = jnp.where(first, ms0, ms1) * (1.0 / NA_HEAD_DIM)
        y = o * lax.rsqrt(ms + NORM_EPS) * gain
        y = y * sg_ref[0, pl.ds(q0, GRID_W), :].astype(F32)
        o_ref[0, pl.ds(q0, GRID_W), :] = y.astype(BF16)

    per_iter = min(NA_ROWS_PER_ITER, rows)

    def body(i, carry):
        rs_ = [scores(i * per_iter + u) for u in range(per_iter)]
        ps = [softmax(s) for _, _, s in rs_]
        os_ = [weighted_values(k0, p) for (_, k0, _), p in zip(rs_, ps)]
        for (q0, _, _), o2 in zip(rs_, os_):
            finish(q0, o2)
        return carry

    lax.fori_loop(0, rows // per_iter, body, 0)


def _attn(rpb_flat, aq, ak, av, ag, attn_norm):
    b, s, _ = aq.shape
    kh = min(NA_ROWS, s // GRID_W)
    assert kh % 2 == 0
    blk = pl.BlockSpec((1, s, V7X_LANES), lambda hp, i: (i, 0, hp))
    return pl.pallas_call(
        _attn_kernel,
        grid=(NA_PAIRS, b),
        in_specs=[pl.BlockSpec(memory_space=pltpu.SMEM), blk, blk, blk, blk,
                  pl.BlockSpec((1, V7X_LANES), lambda hp, i: (0, hp))],
        out_specs=blk,
        out_shape=jax.ShapeDtypeStruct((b, s, NA_WIDTH), BF16),
        scratch_shapes=[pltpu.VMEM((kh, 2 * GRID_W, kh * GRID_W), F32)],
        compiler_params=pltpu.CompilerParams(
            dimension_semantics=("parallel", "arbitrary"), vmem_limit_bytes=V7X_VMEM_LIMIT),
        name="attn",
    )(rpb_flat, aq, ak, av, ag, attn_norm)


def _block_ref(b, n, rev):
    c, w = b.shape
    pos = n if rev else n - 1
    if 2 * n >= 8:
        b3 = b.reshape(c // (2 * n), 2 * n, w)
        return jnp.broadcast_to(b3[:, pos:pos + 1, :], b3.shape).reshape(c, w)
    row = lax.broadcasted_iota(jnp.int32, b.shape, 0)
    if n == 1:
        if rev:
            return jnp.where(row % 2 == 0, pltpu.roll(b, c - 1, 0), b)
        return jnp.where(row % 2 == 1, pltpu.roll(b, 1, 0), b)
    assert n == 2
    if rev:
        c1 = jnp.where(row % 2 == 1, pltpu.roll(b, 1, 0), b)
        return jnp.where(row % 4 < 2, pltpu.roll(c1, c - 2, 0), c1)
    c1 = jnp.where(row % 2 == 0, pltpu.roll(b, c - 1, 0), b)
    return jnp.where(row % 4 >= 2, pltpu.roll(c1, 2, 0), c1)


def _level_decay(b, n, rev):
    c, w = b.shape
    if n < 8:
        return jnp.exp2(-jnp.abs(b - _block_ref(b, n, rev)))
    b3 = b.reshape(c // (2 * n), 2 * n, w)
    lo, hi = b3[:, :n], b3[:, n:]
    if rev:
        ref = b3[:, n:n + 1]
        parts = [jnp.exp2(lo - ref), jnp.exp2(ref - hi)]
    else:
        ref = b3[:, n - 1:n]
        parts = [jnp.exp2(ref - lo), jnp.exp2(hi - ref)]
    return jnp.concatenate(parts, axis=1).reshape(c, w)


def _base_ref(b3, rev):
    base = b3.shape[1]
    pos = base // 2 if rev else base // 2 - 1
    return b3[:, pos:pos + 1]


def _level_ids(c, rev, base):
    t_idx = lax.broadcasted_iota(jnp.int32, (c, c), 0)
    s_idx = lax.broadcasted_iota(jnp.int32, (c, c), 1)
    x = t_idx ^ s_idx
    lev = jnp.ones((c, c), jnp.int32)
    n = 2 * base
    while n < c:
        lev = lev + (x >= n).astype(jnp.int32)
        n *= 2
    before = (t_idx < s_idx) if rev else (t_idx > s_idx)
    lev = jnp.where(before & (x >= base), lev, -1)
    return jnp.where((x < base) & (before | (t_idx == s_idx)), 0, lev)


def _hgrn_scores(q, kk, b, lev, rev, base):
    c, w = q.shape
    nt = (((1,), (1,)), ((), ()))

    if base == 1:
        scores, growth, decay = lax.dot_general(q, kk, nt, preferred_element_type=F32), None, None
    else:
        b3 = b.reshape(c // base, base, w)
        d = (b3 - _base_ref(b3, rev)).reshape(c, w)
        growth, decay = jnp.exp2(-d).astype(BF16), jnp.exp2(d).astype(BF16)
        scores = lax.dot_general(decay * q, growth * kk, nt, preferred_element_type=F32)
    scores = jnp.where(lev == 0, scores, 0.0)
    n, li = base, 1
    while n < c:
        e = _level_decay(b, n, rev).astype(BF16)
        sl = lax.dot_general(e * q, e * kk, nt, preferred_element_type=F32)
        scores = jnp.where(lev == li, sl, scores)
        n, li = 2 * n, li + 1
    return scores.astype(BF16), growth, decay


def _hgrn_output(scores, q, kk, v, b, st, rev, growth, decay):
    c, w = q.shape
    btot = b[0:1, :] if rev else b[c - 1:c, :]
    if decay is None:
        from_start, to_end = jnp.exp2(b).astype(BF16), jnp.exp2(btot - b).astype(BF16)
    else:
        bmid = _base_ref(b.reshape(1, c, w), rev).reshape(1, w)
        from_start = decay * jnp.exp2(bmid).astype(BF16)
        to_end = growth * jnp.exp2(btot - bmid).astype(BF16)
    lhs = jnp.concatenate([scores, from_start * q], axis=1)
    rhs = jnp.concatenate([v, st.astype(BF16)], axis=0)
    o = jnp.dot(lhs, rhs, preferred_element_type=F32)

    kd = to_end * kk
    upd = lax.dot_general(kd, v, (((0,), (0,)), ((), ())), preferred_element_type=F32)
    keep = jnp.broadcast_to(jnp.exp2(btot), (w, w)).T
    return o, st * keep + upd


def _hgrn_kernel(qf_ref, vf_ref, lff_ref, kf_ref, qb_ref, vb_ref, lfb_ref, kb_ref,
                 of_ref, ob_ref, st_ref, b_ref):
    @pl.when(pl.program_id(1) == 0)
    def _():
        st_ref[...] = jnp.zeros_like(st_ref)

    c = HG_CHUNK
    n_sub = qf_ref.shape[1] // c
    t_idx = lax.broadcasted_iota(jnp.int32, (c, c), 0)
    u_idx = lax.broadcasted_iota(jnp.int32, (c, c), 1)
    dirs = ((0, False, qf_ref, kf_ref, vf_ref, lff_ref, of_ref),
            (1, True, qb_ref, kb_ref, vb_ref, lfb_ref, ob_ref))

    widest = jnp.zeros((c // HG_BASE, 1, lff_ref.shape[2]), F32)
    for di, rev, _, _, _, lf_ref, _ in dirs:
        tri = ((u_idx >= t_idx) if rev else (u_idx <= t_idx)).astype(F32).astype(BF16)
        for j in range(n_sub):
            rows = slice(j * c, (j + 1) * c)
            bj = jnp.dot(tri, lf_ref[0, rows, :], preferred_element_type=F32) * LOG2E
            b_ref[di, rows, :] = bj
            b3 = bj.reshape(c // HG_BASE, HG_BASE, bj.shape[1])
            ref = _base_ref(b3, rev)
            widest = jnp.maximum(widest, jnp.maximum(jnp.abs(b3[:, 0:1] - ref),
                                                     jnp.abs(b3[:, HG_BASE - 1:HG_BASE] - ref)))
    fast = jnp.max(widest) <= HG_MAX_LOG2_RANGE

    def run(base):
        lev = {rev: _level_ids(c, rev, base) for rev in (False, True)}

        def body(j, carry):
            work = []
            for di, rev, q_ref, k_ref, v_ref, _, o_ref in dirs:
                r0 = ((n_sub - 1 - j) if rev else j) * c
                if not isinstance(j, int):
                    r0 = pl.multiple_of(r0, c)
                for h in range(HG_HEADS):
                    cs = slice(h * HG_HEAD_DIM, (h + 1) * HG_HEAD_DIM)
                    ld = lambda ref: ref[0, pl.ds(r0, c), cs]
                    q, kk, bb = ld(q_ref), ld(k_ref), b_ref[di, pl.ds(r0, c), cs]
                    scores, growth, decay = _hgrn_scores(q, kk, bb, lev[rev], rev, base)
                    work.append((di, rev, h, r0, cs, q, kk, ld(v_ref), bb, scores, growth, decay, o_ref))
            for di, rev, h, r0, cs, q, kk, v, bb, scores, growth, decay, o_ref in work:
                o, st = _hgrn_output(scores, q, kk, v, bb, st_ref[di, h], rev, growth, decay)
                st_ref[di, h] = st
                o_ref[0, pl.ds(r0, c), cs] = o.astype(o_ref.dtype)
            return carry

        if base == 1:
            lax.fori_loop(0, n_sub, body, 0)
        else:
            for j in range(n_sub):
                body(j, 0)

    pl.when(fast)(lambda: run(HG_BASE))
    pl.when(jnp.logical_not(fast))(lambda: run(1))


def _hgrn(hq, hv, lff, kf, lfb, kb):
    b, s, w = hq.shape
    ns = s // HG_STEP
    fwd = pl.BlockSpec((1, HG_STEP, w), lambda i, j: (i, j, 0))
    bwd = pl.BlockSpec((1, HG_STEP, w), lambda i, j: (i, ns - 1 - j, 0))
    sds = jax.ShapeDtypeStruct((b, s, w), BF16)
    return pl.pallas_call(
        _hgrn_kernel,
        grid=(b, ns),
        in_specs=[fwd, fwd, fwd, fwd, bwd, bwd, bwd, bwd],
        out_specs=[fwd, bwd],
        out_shape=[sds, sds],
        scratch_shapes=[pltpu.VMEM((2, HG_HEADS, HG_HEAD_DIM, HG_HEAD_DIM), F32),
                        pltpu.VMEM((2, HG_STEP, w), F32)],
        compiler_params=pltpu.CompilerParams(
            dimension_semantics=("parallel", "arbitrary"), vmem_limit_bytes=V7X_VMEM_LIMIT),
        name="hgrn",
    )(hq, hv, lff, kf, hq, hv, lfb, kb)


def _rms(x, g):
    ms = jnp.mean(x * x, axis=-1, keepdims=True)
    return x * lax.rsqrt(ms + NORM_EPS) * g


def _out_kernel(a_ref, of_ref, ob_ref, hg_ref, x_ref, p_ref,
                gh_ref, wo_ref, gp_ref, wg_ref, wp_ref, gf_ref, o_ref):
    sub = x_ref.shape[0] // OUT_SPLIT
    tiles = [slice(u * sub, (u + 1) * sub) for u in range(OUT_SPLIT)]
    gh = gh_ref[...]

    def mixed(rows):
        r = of_ref[rows, :].astype(F32) + ob_ref[rows, :].astype(F32)
        parts = [a_ref[rows, :]]
        for h in range(HG_HEADS):
            cs = slice(h * HG_HEAD_DIM, (h + 1) * HG_HEAD_DIM)
            y = _rms(r[:, cs], gh[:, cs]) * hg_ref[rows, cs].astype(F32)
            parts.append(y.astype(BF16))
        return jnp.concatenate(parts, axis=-1)

    cats = [mixed(rows) for rows in tiles]
    pps = [jnp.dot(p_ref[rows, :].astype(BF16), wp_ref[...], preferred_element_type=F32)
           for rows in tiles]
    h1s = [x_ref[rows, :] + jnp.dot(cat, wo_ref[...], preferred_element_type=F32)
           for rows, cat in zip(tiles, cats)]
    gates = [_sigmoid(jnp.dot(_rms(h1, gp_ref[...]).astype(BF16), wg_ref[...],
                              preferred_element_type=F32)) for h1 in h1s]
    for rows, h1, gate, pp in zip(tiles, h1s, gates, pps):
        o_ref[rows, :] = _rms(h1 + gate * pp, gf_ref[...])


def _out(a, o_f, o_b, hg, x2, p2, hgrn_norm, w_out, ln_ple, w_pg, w_pp, ln_final):
    n = x2.shape[0]
    row = lambda w: pl.BlockSpec((OUT_TM, w), lambda i: (i, 0))
    full = lambda r, c: pl.BlockSpec((r, c), lambda i: (0, 0), pipeline_mode=pl.Buffered(1))
    return pl.pallas_call(
        _out_kernel,
        grid=(n // OUT_TM,),
        in_specs=[row(NA_WIDTH), row(HG_WIDTH), row(HG_WIDTH), row(HG_WIDTH),
                  row(D_MODEL), row(PLE_DIM),
                  full(1, HG_WIDTH), full(D_MODEL, D_MODEL), full(1, D_MODEL),
                  full(D_MODEL, D_MODEL), full(PLE_DIM, D_MODEL), full(1, D_MODEL)],
        out_specs=row(D_MODEL),
        out_shape=jax.ShapeDtypeStruct((n, D_MODEL), F32),
        compiler_params=pltpu.CompilerParams(
            dimension_semantics=("parallel",), vmem_limit_bytes=V7X_VMEM_LIMIT),
        name="out",
    )(a, o_f, o_b, hg, x2, p2, hgrn_norm, w_out, ln_ple, w_pg, w_pp, ln_final)


def kernel(x, p, ln_mix, w_in, rpb, lb_logits, attn_norm, hgrn_norm, w_out, ln_ple, w_pg, w_pp, ln_final):
    depth = p.shape[0]
    assert depth == 1 and lb_logits.shape[0] == 2
    b, s, d = x.shape
    n = b * s
    rows = s // GRID_W
    assert d == D_MODEL and n % PROJ_TM == 0 and n % OUT_TM == 0 and s % HG_STEP == 0
    assert rows % min(NA_ROWS_PER_ITER, rows) == 0 and rows >= NA_ROWS

    x2 = x.reshape(n, d)
    (aq, ak, av, ag, hq, lff, kf, lfb, kb, hv, hg) = _proj(
        x2, ln_mix[0:1], w_in[0].astype(BF16), lb_logits)

    r3 = lambda t: t.reshape(b, s, t.shape[-1])
    a = _attn(rpb[0].reshape(-1), r3(aq), r3(ak), r3(av), r3(ag), attn_norm[0:1])
    o_f, o_b = _hgrn(r3(hq), r3(hv), r3(lff), r3(kf), r3(lfb), r3(kb))

    out = _out(a.reshape(n, NA_WIDTH), o_f.reshape(n, HG_WIDTH), o_b.reshape(n, HG_WIDTH), hg,
               x2, p[0].reshape(n, PLE_DIM), hgrn_norm[0:1], w_out[0].astype(BF16),
               ln_ple[0:1], w_pg[0].astype(BF16), w_pp[0].astype(BF16), ln_final.reshape(1, d))
    return out.reshape(b, s, d)
```
